```python
import math
import jax
import jax.numpy as jnp
from jax import lax
import numpy as np


D_MODEL = 1024
BATCH = 8
SEQ = 4096
DEPTH = 4

N_MIXERS = 2
N_A_LAYERS = (DEPTH + 1) // 2
N_B_LAYERS = DEPTH // 2
N_VRES = max(N_B_LAYERS - 1, 0)

DEEPNORM_ALPHA = (2 * DEPTH) ** 0.25
DEEPNORM_BETA = (8 * DEPTH) ** -0.25
LN_EPS = 1e-5

A_HEAD_DIM = 64
A_HEADS = D_MODEL // A_HEAD_DIM
MOBA_BLOCK = 256
MOBA_TOPK = 3
MOBA_QCHUNK = 128
N_BUCKETS = 32
MAX_DISTANCE = 128

B_HEAD_DIM = 64
B_HEADS = D_MODEL // B_HEAD_DIM
DECAY_LORA = 64
AAA_LORA = 64
MV_LORA = 32
GATE_LORA = 160
GN_EPS = 64e-5

N_EXPERTS = 64
N_GROUPS = 8
TOPK_GROUPS = 4
TOP_K = 8
D_EXPERT = 256
ROUTED_SCALE = 2.5
MOE_BLOCK = 512

kernel_name = 'moba_rwkv7_moe_deepnorm_hybrid'


def _layer_norm(x, g, b):
    xf = x.astype(jnp.float32)
    mu = jnp.mean(xf, -1, keepdims=True)
    var = jnp.mean(jnp.square(xf - mu), -1, keepdims=True)
    return ((xf - mu) * lax.rsqrt(var + LN_EPS) * g + b).astype(x.dtype)


def _rel_bucket(dist):
    n = jnp.maximum(dist, 0)
    max_exact = N_BUCKETS // 2
    nf = jnp.maximum(n, 1).astype(jnp.float32)
    large = max_exact + (jnp.log(nf / max_exact) / math.log(MAX_DISTANCE / max_exact)
                         * (N_BUCKETS - max_exact)).astype(jnp.int32)
    return jnp.where(n < max_exact, n, jnp.minimum(large, N_BUCKETS - 1))


def _attend(s, v):
    m = jnp.max(s, -1, keepdims=True)
    p = jnp.exp(s - m)
    l = jnp.sum(p, -1, keepdims=True)
    o = jnp.einsum('...qk,...kd->...qd', p, v) / l
    return o, (m + jnp.log(l))[..., 0]


def _group_rows(group_ids, n_groups, block):
    n = group_ids.shape[0]
    n_blocks = -(-n // block) + n_groups
    n_rows = n_blocks * block
    order = jnp.argsort(group_ids).astype(jnp.int32)
    g_sorted = group_ids[order]
    counts = jnp.bincount(group_ids, length=n_groups + 1)[:n_groups]
    padded = (counts + block - 1) // block * block
    pend = jnp.cumsum(padded)
    pstart = pend - padded
    sstart = jnp.cumsum(counts) - counts
    gc = jnp.minimum(g_sorted, n_groups - 1)
    row = jnp.where(g_sorted < n_groups, pstart[gc] + jnp.arange(n) - sstart[gc], n_rows)
    row_item = jnp.full((n_rows,), n, jnp.int32).at[row].set(order, mode='drop')
    block_group = jnp.minimum(jnp.searchsorted(pend, jnp.arange(n_blocks) * block, side='right'), n_groups - 1)
    return row_item, block_group


def _moba_head(q, k, v, bias_h):
    sp, dh = q.shape
    nblk = sp // MOBA_BLOCK
    qb = q.reshape(nblk, MOBA_BLOCK, dh)
    kb = k.reshape(nblk, MOBA_BLOCK, dh)
    vb = v.reshape(nblk, MOBA_BLOCK, dh)
    off = jnp.arange(MOBA_BLOCK)
    rel = off[:, None] - off[None, :]
    s_self = jnp.einsum('nqd,nkd->nqk', qb, kb) + bias_h[_rel_bucket(rel)]
    s_self = jnp.where(rel >= 0, s_self, -jnp.inf)
    o_self, lse_self = _attend(s_self, vb)
    o_self = o_self.reshape(sp, dh)
    lse_self = lse_self.reshape(sp)
    k_eff = min(MOBA_TOPK, nblk - 1)
    if k_eff == 0:
        return o_self
    pos = jnp.arange(sp)
    qblk = pos // MOBA_BLOCK
    gate = q @ jnp.mean(kb, axis=1).T
    gate = jnp.where(jnp.arange(nblk)[None, :] < qblk[:, None], gate, -jnp.inf)
    _, sel = lax.top_k(gate, k_eff)
    valid = jnp.arange(k_eff)[None, :] < qblk[:, None]
    n_sel = sp * k_eff
    blk_flat = jnp.where(valid, sel, nblk).reshape(n_sel)
    row_sel, chunk_blk = _group_rows(blk_flat, nblk, MOBA_QCHUNK)
    n_rows = row_sel.shape[0]
    n_chunks = chunk_blk.shape[0]
    row_q = jnp.where(row_sel < n_sel, row_sel // k_eff, sp)
    q_pad = jnp.concatenate([q, jnp.zeros((1, dh), q.dtype)], 0)
    qg = q_pad[row_q].reshape(n_chunks, MOBA_QCHUNK, dh)
    kg = kb[chunk_blk]
    vg = vb[chunk_blk]
    qpos = row_q.reshape(n_chunks, MOBA_QCHUNK)
    kpos = chunk_blk[:, None] * MOBA_BLOCK + off[None, :]
    s = jnp.einsum('ncd,ntd->nct', qg, kg) + bias_h[_rel_bucket(qpos[:, :, None] - kpos[:, None, :])]
    o, lse = _attend(s, vg)
    o_sel = jnp.zeros((n_sel + 1, dh), o.dtype).at[row_sel].set(o.reshape(n_rows, dh))[:n_sel]
    lse_sel = jnp.full((n_sel + 1,), -jnp.inf, jnp.float32).at[row_sel].set(lse.reshape(n_rows))[:n_sel]
    lse_all = jnp.concatenate([lse_sel.reshape(sp, k_eff), lse_self[:, None]], 1)
    o_all = jnp.concatenate([o_sel.reshape(sp, k_eff, dh), o_self[:, None]], 1)
    wts = jax.nn.softmax(lse_all, axis=-1)
    return jnp.einsum('sj,sjd->sd', wts, o_all)


def _moba_mixer(x, w_qkv, w_o, rel_bias):
    b, s, d = x.shape
    qkv = (x @ w_qkv).astype(jnp.float32).reshape(b, s, 3, A_HEADS, A_HEAD_DIM)
    qkv = jnp.transpose(qkv, (2, 0, 3, 1, 4))
    sp = -(-s // MOBA_BLOCK) * MOBA_BLOCK
    qkv = jnp.pad(qkv, ((0, 0), (0, 0), (0, 0), (0, sp - s), (0, 0)))
    q = qkv[0] * (A_HEAD_DIM ** -0.5)
    k, v = qkv[1], qkv[2]
    bias_hd = rel_bias.astype(jnp.float32).T
    per_head = jax.vmap(_moba_head)
    o = lax.map(lambda t: per_head(t[0], t[1], t[2], bias_hd), (q, k, v))
    o = jnp.transpose(o[:, :, :s], (0, 2, 1, 3)).reshape(b, s, d).astype(x.dtype)
    return o @ w_o


def _wkv7_scan(r, w, k, v, a, b):
    bsz, t, h, n = r.shape

    def step(state, inp):
        r_t, w_t, k_t, v_t, a_t, b_t = inp
        sa = jnp.einsum('bhvk,bhk->bhv', state, a_t)
        state = (state * w_t[:, :, None, :] + sa[..., None] * b_t[:, :, None, :]
                 + v_t[..., None] * k_t[:, :, None, :])
        return state, jnp.einsum('bhvk,bhk->bhv', state, r_t)

    xs = tuple(jnp.moveaxis(z, 1, 0) for z in (r, w, k, v, a, b))
    _, ys = lax.scan(step, jnp.zeros((bsz, h, n, n), jnp.float32), xs)
    return jnp.moveaxis(ys, 0, 1)


def _rwkv7_mixer(x, v_first, mix, w_rkv, w0, w1, w2, a0, a1, a2, g1, g2,
                 k_k, k_a, r_k, gn_g, gn_b, w_o, vres):
    b, t, d = x.shape
    xx = jnp.pad(x, ((0, 0), (1, 0), (0, 0)))[:, :-1] - x
    r = (x + xx * mix[0]) @ w_rkv[0]
    k = (x + xx * mix[1]) @ w_rkv[1]
    xv = x + xx * mix[2]
    v = xv @ w_rkv[2]
    w_log = -jax.nn.softplus(-(w0 + jnp.tanh((x + xx * mix[3]) @ w1) @ w2)) - 0.5
    a = jax.nn.sigmoid(a0 + ((x + xx * mix[4]) @ a1) @ a2)
    g = jax.nn.sigmoid((x + xx * mix[5]) @ g1) @ g2
    if vres is None:
        v_first = v
    else:
        v0, v1, v2 = vres
        v = v + (v_first - v) * jax.nn.sigmoid(v0 + (xv @ v1) @ v2)

    def heads(z):
        return z.astype(jnp.float32).reshape(b, t, B_HEADS, B_HEAD_DIM)

    kk = heads(k * k_k)
    kk = kk / jnp.maximum(jnp.sqrt(jnp.sum(kk * kk, -1, keepdims=True)), 1e-12)
    a_h = heads(a)
    k_h = heads(k * (1.0 + (a - 1.0) * k_a))
    r_h = heads(r)
    v_h = heads(v)
    decay = jnp.exp(-jnp.exp(heads(w_log)))
    y = _wkv7_scan(r_h, decay, k_h, v_h, -kk, kk * a_h)
    mu = jnp.mean(y, -1, keepdims=True)
    var = jnp.mean(jnp.square(y - mu), -1, keepdims=True)
    y = ((y - mu) * lax.rsqrt(var + GN_EPS)).reshape(b, t, d) * gn_g + gn_b
    bonus = jnp.sum(r_h * k_h * r_k, -1, keepdims=True) * v_h
    y = (y + bonus.reshape(b, t, d)).astype(x.dtype)
    return (y * g) @ w_o, v_first


def _routed_experts(xt, top_e, gate, w_gate, w_up, w_down):
    t, d = xt.shape
    n_assign = t * TOP_K
    row_item, block_expert = _group_rows(top_e.reshape(n_assign), N_EXPERTS, MOE_BLOCK)
    valid = row_item < n_assign
    row_tok = jnp.where(valid, row_item // TOP_K, t).reshape(-1, MOE_BLOCK)
    row_gate = jnp.where(valid, gate.reshape(n_assign)[jnp.minimum(row_item, n_assign - 1)], 0.0)
    row_gate = row_gate.reshape(-1, MOE_BLOCK).astype(xt.dtype)
    x_pad = jnp.concatenate([xt, jnp.zeros((1, d), xt.dtype)], 0)

    def body(acc, blk):
        tok, g, e = blk
        xe = x_pad[tok]
        h = jax.nn.silu(xe @ w_gate[e]) * (xe @ w_up[e])
        return acc.at[tok].add((h @ w_down[e]) * g[:, None]), None

    acc, _ = lax.scan(body, jnp.zeros((t + 1, d), xt.dtype), (row_tok, row_gate, block_expert))
    return acc[:t]


def _moe_ffn(x, w_router, router_bias, w_gate, w_up, w_down, sh_gate, sh_up, sh_down):
    b, s, d = x.shape
    xt = x.reshape(b * s, d)
    scores = jax.nn.sigmoid(xt.astype(jnp.float32) @ w_router.astype(jnp.float32))
    biased = scores + router_bias.astype(jnp.float32)
    per_group = biased.reshape(-1, N_GROUPS, N_EXPERTS // N_GROUPS)
    group_score = jnp.sum(lax.top_k(per_group, 2)[0], -1)
    _, top_g = lax.top_k(group_score, TOPK_GROUPS)
    group_keep = jnp.any(top_g[:, :, None] == jnp.arange(N_GROUPS)[None, None, :], axis=1)
    expert_keep = jnp.repeat(group_keep, N_EXPERTS // N_GROUPS, axis=1)
    _, top_e = lax.top_k(jnp.where(expert_keep, biased, -jnp.inf), TOP_K)
    gate = jnp.take_along_axis(scores, top_e, axis=1)
    gate = gate / jnp.sum(gate, -1, keepdims=True) * ROUTED_SCALE
    routed = _routed_experts(xt, top_e, gate, w_gate, w_up, w_down)
    shared = (jax.nn.silu(xt @ sh_gate) * (xt @ sh_up)) @ sh_down
    return (routed + shared).reshape(b, s, d)


def setup_inputs(seed: int = 0) -> dict:
    key = jax.random.key(seed)
    ks = jax.random.split(key, 64)
    counter = iter(range(64))
    f32 = jnp.float32

    def nrm(shape, scale):
        return jax.random.normal(ks[next(counter)], shape, f32) * scale

    def unif(shape, lo, hi):
        return jax.random.uniform(ks[next(counter)], shape, f32, lo, hi)

    d, L = D_MODEL, DEPTH
    na, nb, nv = N_A_LAYERS, N_B_LAYERS, N_VRES
    sd = d ** -0.5
    beta = DEEPNORM_BETA
    x = nrm((BATCH, SEQ, d), 1.0)
    moba_w_qkv = jnp.concatenate([nrm((na, d, 2 * d), sd), nrm((na, d, d), sd * beta)], axis=-1)
    moba_w_o = nrm((na, d, d), sd * beta)
    rel_bias = nrm((N_BUCKETS, A_HEADS), 0.5)
    rwkv_mix = unif((nb, 6, d), 0.0, 1.0)
    rwkv_w_rkv = jnp.concatenate([nrm((nb, 2, d, d), sd), nrm((nb, 1, d, d), sd * beta)], axis=1)
    rwkv_w0 = unif((nb, d), -6.5, -1.5)
    rwkv_w1 = nrm((nb, d, DECAY_LORA), sd)
    rwkv_w2 = nrm((nb, DECAY_LORA, d), 0.1 * DECAY_LORA ** -0.5)
    rwkv_a0 = nrm((nb, d), 0.1)
    rwkv_a1 = nrm((nb, d, AAA_LORA), sd)
    rwkv_a2 = nrm((nb, AAA_LORA, d), 0.1 * AAA_LORA ** -0.5)
    rwkv_v0 = 1.0 + nrm((nv, d), 0.1)
    rwkv_v1 = nrm((nv, d, MV_LORA), sd)
    rwkv_v2 = nrm((nv, MV_LORA, d), 0.1 * MV_LORA ** -0.5)
    rwkv_g1 = nrm((nb, d, GATE_LORA), sd)
    rwkv_g2 = nrm((nb, GATE_LORA, d), GATE_LORA ** -0.5)
    rwkv_k_k = 0.85 + nrm((nb, d), 0.05)
    rwkv_k_a = 1.0 + nrm((nb, d), 0.05)
    rwkv_r_k = nrm((nb, B_HEADS, B_HEAD_DIM), 0.1)
    rwkv_gn_g = 1.0 + nrm((nb, d), 0.05)
    rwkv_gn_b = nrm((nb, d), 0.02)
    rwkv_w_o = nrm((nb, d, d), sd * beta)
    moe_w_router = nrm((L, d, N_EXPERTS), sd)
    moe_router_bias = nrm((L, N_EXPERTS), 0.01)
    moe_w_gate = nrm((L, N_EXPERTS, d, D_EXPERT), sd)
    moe_w_up = nrm((L, N_EXPERTS, d, D_EXPERT), sd)
    moe_w_down = nrm((L, N_EXPERTS, D_EXPERT, d), D_EXPERT ** -0.5 * beta)
    moe_sh_gate = nrm((L, d, D_EXPERT), sd)
    moe_sh_up = nrm((L, d, D_EXPERT), sd)
    moe_sh_down = nrm((L, D_EXPERT, d), D_EXPERT ** -0.5 * beta)
    ln_mix_g = 1.0 + nrm((L, d), 0.05)
    ln_mix_b = nrm((L, d), 0.02)
    ln_ffn_g = 1.0 + nrm((L, d), 0.05)
    ln_ffn_b = nrm((L, d), 0.02)
    return {
        'x': x, 'moba_w_qkv': moba_w_qkv, 'moba_w_o': moba_w_o, 'rel_bias': rel_bias,
        'rwkv_mix': rwkv_mix, 'rwkv_w_rkv': rwkv_w_rkv, 'rwkv_w0': rwkv_w0, 'rwkv_w1': rwkv_w1,
        'rwkv_w2': rwkv_w2, 'rwkv_a0': rwkv_a0, 'rwkv_a1': rwkv_a1, 'rwkv_a2': rwkv_a2,
        'rwkv_v0': rwkv_v0, 'rwkv_v1': rwkv_v1, 'rwkv_v2': rwkv_v2, 'rwkv_g1': rwkv_g1,
        'rwkv_g2': rwkv_g2, 'rwkv_k_k': rwkv_k_k, 'rwkv_k_a': rwkv_k_a, 'rwkv_r_k': rwkv_r_k,
        'rwkv_gn_g': rwkv_gn_g, 'rwkv_gn_b': rwkv_gn_b, 'rwkv_w_o': rwkv_w_o,
        'moe_w_router': moe_w_router, 'moe_router_bias': moe_router_bias, 'moe_w_gate': moe_w_gate,
        'moe_w_up': moe_w_up, 'moe_w_down': moe_w_down, 'moe_sh_gate': moe_sh_gate,
        'moe_sh_up': moe_sh_up, 'moe_sh_down': moe_sh_down,
        'ln_mix_g': ln_mix_g, 'ln_mix_b': ln_mix_b, 'ln_ffn_g': ln_ffn_g, 'ln_ffn_b': ln_ffn_b,
    }


def reference(x, moba_w_qkv, moba_w_o, rel_bias, rwkv_mix, rwkv_w_rkv, rwkv_w0, rwkv_w1, rwkv_w2,
              rwkv_a0, rwkv_a1, rwkv_a2, rwkv_v0, rwkv_v1, rwkv_v2, rwkv_g1, rwkv_g2, rwkv_k_k,
              rwkv_k_a, rwkv_r_k, rwkv_gn_g, rwkv_gn_b, rwkv_w_o, moe_w_router, moe_router_bias,
              moe_w_gate, moe_w_up, moe_w_down, moe_sh_gate, moe_sh_up, moe_sh_down,
              ln_mix_g, ln_mix_b, ln_ffn_g, ln_ffn_b):
    v_first = None
    for i in range(DEPTH):
        j = i // N_MIXERS
        if i % N_MIXERS == 0:
            h = _moba_mixer(x, moba_w_qkv[j], moba_w_o[j], rel_bias)
        else:
            vres = None if j == 0 else (rwkv_v0[j - 1], rwkv_v1[j - 1], rwkv_v2[j - 1])
            h, v_first = _rwkv7_mixer(x, v_first, rwkv_mix[j], rwkv_w_rkv[j], rwkv_w0[j], rwkv_w1[j],
                                      rwkv_w2[j], rwkv_a0[j], rwkv_a1[j], rwkv_a2[j], rwkv_g1[j],
                                      rwkv_g2[j], rwkv_k_k[j], rwkv_k_a[j], rwkv_r_k[j],
                                      rwkv_gn_g[j], rwkv_gn_b[j], rwkv_w_o[j], vres)
        x = _layer_norm(DEEPNORM_ALPHA * x + h, ln_mix_g[i], ln_mix_b[i])
        f = _moe_ffn(x, moe_w_router[i], moe_router_bias[i], moe_w_gate[i], moe_w_up[i], moe_w_down[i],
                     moe_sh_gate[i], moe_sh_up[i], moe_sh_down[i])
        x = _layer_norm(DEEPNORM_ALPHA * x + f, ln_ffn_g[i], ln_ffn_b[i])
    return x
```

```python
import functools
import math

import jax
import jax.numpy as jnp
import numpy as np
from jax import lax
from jax.experimental import pallas as pl
from jax.experimental.pallas import tpu as pltpu

F32 = jnp.float32
BF16 = jnp.bfloat16

D_MODEL = 1024
DEPTH = 4
ALPHA = (2 * DEPTH) ** 0.25
LN_EPS = 1e-5

HEAD_DIM = 64
N_HEADS = D_MODEL // HEAD_DIM
LANES = 128
N_PAIRS = D_MODEL // LANES
MOBA_BLOCK = 256
MOBA_TOPK = 3
N_BUCKETS = 32
MAX_DISTANCE = 128
MASKED = -1e30

GN_EPS = 64e-5
WKV_CHUNK = 64

N_EXPERTS = 64
N_GROUPS = 8
TOPK_GROUPS = 4
TOP_K = 8
D_EXPERT = 256
ROUTED_SCALE = 2.5
MOE_ROWS = 512

VMEM_LIMIT = 56 * 1024 * 1024
ROW_BLOCK = 512

_CONTRACT_LAST = (((1,), (1,)), ((), ()))
_CONTRACT_FIRST = (((0,), (0,)), ((), ()))


def _params(*sem):
    return pltpu.CompilerParams(dimension_semantics=sem, vmem_limit_bytes=VMEM_LIMIT)


def _dot(a, b):
    return jnp.dot(a, b, preferred_element_type=F32)


def _dot_t(a, b):
    return lax.dot_general(a, b, _CONTRACT_LAST, preferred_element_type=F32)


def _split3(x):
    hi = x.astype(BF16)
    r1 = x - hi.astype(F32)
    mid = r1.astype(BF16)
    lo = (r1 - mid.astype(F32)).astype(BF16)
    return hi, mid, lo


def _layer_norm_rows(y, g, b):
    mu = jnp.mean(y, axis=-1, keepdims=True)
    yc = y - mu
    var = jnp.mean(yc * yc, axis=-1, keepdims=True)
    return yc * lax.rsqrt(var + LN_EPS) * g + b


def _mm_kernel(x_ref, w_ref, o_ref):
    x = x_ref[...].astype(BF16)
    o_ref[...] = _dot(x, w_ref[...]).astype(o_ref.dtype)


def _matmul(x, w, out_dtype):
    m, k = x.shape
    n = w.shape[1]
    return pl.pallas_call(
        _mm_kernel,
        grid=(m // ROW_BLOCK,),
        in_specs=[pl.BlockSpec((ROW_BLOCK, k), lambda i: (i, 0)),
                  pl.BlockSpec((k, n), lambda i: (0, 0))],
        out_specs=pl.BlockSpec((ROW_BLOCK, n), lambda i: (i, 0)),
        out_shape=jax.ShapeDtypeStruct((m, n), out_dtype),
        compiler_params=_params("parallel"),
    )(x, w)


def _proj_ln_kernel(x_ref, a_ref, w_ref, g_ref, b_ref, o_ref):
    h = _dot(a_ref[...].astype(BF16), w_ref[...])
    o_ref[...] = _layer_norm_rows(ALPHA * x_ref[...] + h, g_ref[...], b_ref[...])


def _proj_gate_ln_kernel(x_ref, a_ref, m_ref, w_ref, g_ref, b_ref, o_ref):
    h = _dot((a_ref[...] * m_ref[...]).astype(BF16), w_ref[...])
    o_ref[...] = _layer_norm_rows(ALPHA * x_ref[...] + h, g_ref[...], b_ref[...])


def _proj_ln(x, a, w, g, b, mult=None):
    t, d = x.shape
    k = a.shape[1]
    row = lambda i: (i, 0)
    fix = lambda i: (0, 0)
    acts = [a] if mult is None else [a, mult]
    kern = _proj_ln_kernel if mult is None else _proj_gate_ln_kernel
    return pl.pallas_call(
        kern,
        grid=(t // ROW_BLOCK,),
        in_specs=[pl.BlockSpec((ROW_BLOCK, d), row)]
        + [pl.BlockSpec((ROW_BLOCK, k), row) for _ in acts]
        + [pl.BlockSpec((k, d), fix), pl.BlockSpec((1, d), fix), pl.BlockSpec((1, d), fix)],
        out_specs=pl.BlockSpec((ROW_BLOCK, d), row),
        out_shape=jax.ShapeDtypeStruct((t, d), F32),
        compiler_params=_params("parallel"),
    )(x, *acts, w, g.reshape(1, d), b.reshape(1, d))


def _rel_bucket_np(dist):
    n = np.maximum(dist, 0)
    max_exact = N_BUCKETS // 2
    nf = np.maximum(n, 1).astype(np.float32)
    large = max_exact + (np.log(nf / np.float32(max_exact)) / np.float32(math.log(MAX_DISTANCE / max_exact))
                         * np.float32(N_BUCKETS - max_exact)).astype(np.int32)
    return np.where(n < max_exact, n, np.minimum(large, N_BUCKETS - 1))


def _moba_bias_tables(rel_bias):
    off = np.arange(MOBA_BLOCK)
    rel = off[:, None] - off[None, :]
    assert int(_rel_bucket_np(np.array([MOBA_BLOCK + 1]))[0]) == N_BUCKETS - 1
    bias_hd = rel_bias.astype(F32).T
    far = bias_hd[:, N_BUCKETS - 1][:, None, None]
    own = bias_hd[:, _rel_bucket_np(rel)] - far
    own = jnp.where(jnp.asarray(rel >= 0)[None], own, MASKED)
    prev = bias_hd[:, _rel_bucket_np(rel + MOBA_BLOCK)] - far
    return own, prev


def _moba_kernel(q_ref, k_ref, v_ref, own_ref, prev_ref, o_ref, km_ref, m_ref, acc_ref, *, nblk):
    qb = pl.program_id(2)
    blk = MOBA_BLOCK
    lane = lax.broadcasted_iota(jnp.int32, (blk, LANES), 1)
    lane_sq = lax.broadcasted_iota(jnp.int32, (LANES, LANES), 1)

    @pl.when(qb == 0)
    def _block_means():
        rows = [jnp.sum(k_ref[0, j * blk:(j + 1) * blk, :].astype(F32), axis=0, keepdims=True) * (1.0 / blk)
                for j in range(nblk)]
        km = jnp.concatenate(rows, axis=0)
        lane_k = lax.broadcasted_iota(jnp.int32, (nblk, LANES), 1)
        km0 = jnp.where(lane_k < HEAD_DIM, km, 0.0)
        km1 = jnp.where(lane_k >= HEAD_DIM, km, 0.0)
        z = lambda r: jnp.zeros((r, LANES), F32)
        km_ref[0] = jnp.concatenate([z(HEAD_DIM), km0, z(HEAD_DIM - nblk)], axis=0)
        km_ref[1] = jnp.concatenate([km1, z(LANES - nblk)], axis=0)

    q = q_ref[0]
    heads = []
    for h in range(2):
        own = (lane < HEAD_DIM) if h == 0 else (lane >= HEAD_DIM)
        goff = HEAD_DIM if h == 0 else 0
        kmf = km_ref[h]
        kmh = kmf.astype(BF16)
        kml = (kmf - kmh.astype(F32)).astype(BF16)
        gate = _dot_t(q, kmh) + _dot_t(q, kml)
        jblk = lane - goff
        valid = (jblk >= 0) & (jblk < qb)
        gm = jnp.where(valid, gate, -jnp.inf)
        sel = jnp.zeros((blk, LANES), jnp.bool_)
        for _ in range(MOBA_TOPK):
            mx = jnp.max(gm, axis=-1, keepdims=True)
            first = jnp.min(jnp.where(gm == mx, lane, 1 << 20), axis=-1, keepdims=True)
            pick = (lane == first) & (mx > -jnp.inf)
            sel = sel | pick
            gm = jnp.where(pick, -jnp.inf, gm)
        gsel = jnp.where(valid & jnp.logical_not(sel), MASKED, 0.0)
        q_aug = jnp.where(own, q, gsel.astype(BF16))
        q_own = jnp.where(own, q, jnp.zeros_like(q))
        heads.append((own, goff, q_aug, q_own))

    def attend(h, s, v_aug, first):
        m_cur = jnp.max(s, axis=-1, keepdims=True)
        if first:
            m_new = m_cur
            p = jnp.exp(s - m_new)
            acc_ref[h] = _dot(p.astype(BF16), v_aug)
        else:
            m_prev = m_ref[h][:, :1]
            m_new = jnp.maximum(m_prev, m_cur)
            p = jnp.exp(s - m_new)
            acc_ref[h] = acc_ref[h] * jnp.exp(m_prev - m_new) + _dot(p.astype(BF16), v_aug)
        m_ref[h] = jnp.broadcast_to(m_new, (blk, LANES))

    def kv_block(j):
        start = pl.multiple_of(j * blk, blk)
        return k_ref[0, pl.ds(start, blk), :], v_ref[0, pl.ds(start, blk), :]

    def v_with_ones(own, vj):
        return jnp.where(own, vj, jnp.ones_like(vj))

    kj, vj = kv_block(qb)
    for h, (own, goff, q_aug, q_own) in enumerate(heads):
        attend(h, _dot_t(q_own, kj) + own_ref[h], v_with_ones(own, vj), True)

    def past(j, with_prev_bias):
        kj, vj = kv_block(j)
        for h, (own, goff, q_aug, q_own) in enumerate(heads):
            ej = jnp.where(lane == goff + j, 1.0, 0.0).astype(BF16)
            s = _dot_t(q_aug, jnp.where(own, kj, ej))
            if with_prev_bias:
                s = s + prev_ref[h]
            attend(h, s, v_with_ones(own, vj), False)

    @pl.when(qb >= 1)
    def _prev_block():
        past(qb - 1, True)

    def far_body(j, carry):
        past(j, False)
        return carry

    lax.fori_loop(0, jnp.maximum(qb - 1, 0), far_body, 0)

    a0 = acc_ref[0]
    a1 = acc_ref[1]
    o0 = a0 / pltpu.roll(a0, HEAD_DIM, axis=1)
    o1 = a1 / pltpu.roll(a1, HEAD_DIM, axis=1)
    o_ref[0] = jnp.where(lane < HEAD_DIM, o0, o1).astype(o_ref.dtype)


def _moba_attention(qkv, own_bias, prev_bias):
    b, s, _ = qkv.shape
    assert s % MOBA_BLOCK == 0
    nblk = s // MOBA_BLOCK
    assert nblk <= HEAD_DIM
    return pl.pallas_call(
        functools.partial(_moba_kernel, nblk=nblk),
        grid=(b, N_PAIRS, nblk),
        in_specs=[
            pl.BlockSpec((1, MOBA_BLOCK, LANES), lambda bi, hp, qb: (bi, qb, hp)),
            pl.BlockSpec((1, s, LANES), lambda bi, hp, qb: (bi, 0, N_PAIRS + hp)),
            pl.BlockSpec((1, s, LANES), lambda bi, hp, qb: (bi, 0, 2 * N_PAIRS + hp)),
            pl.BlockSpec((2, MOBA_BLOCK, MOBA_BLOCK), lambda bi, hp, qb: (hp, 0, 0)),
            pl.BlockSpec((2, MOBA_BLOCK, MOBA_BLOCK), lambda bi, hp, qb: (hp, 0, 0)),
        ],
        out_specs=pl.BlockSpec((1, MOBA_BLOCK, LANES), lambda bi, hp, qb: (bi, qb, hp)),
        out_shape=jax.ShapeDtypeStruct((b, s, D_MODEL), BF16),
        scratch_shapes=[pltpu.VMEM((2, LANES, LANES), F32),
                        pltpu.VMEM((2, MOBA_BLOCK, LANES), F32),
                        pltpu.VMEM((2, MOBA_BLOCK, LANES), F32)],
        compiler_params=_params("parallel", "parallel", "arbitrary"),
    )(qkv, qkv, qkv, own_bias, prev_bias)


def _moba_layer(x, w_qkv, w_o, own_bias, prev_bias, ln_g, ln_b):
    b, s, d = x.shape
    xt = x.reshape(b * s, d)
    scale = jnp.concatenate([jnp.full((d,), HEAD_DIM ** -0.5, F32), jnp.ones((2 * d,), F32)])
    qkv = _matmul(xt, (w_qkv * scale).astype(BF16), BF16)
    o = _moba_attention(qkv.reshape(b, s, 3 * d), own_bias, prev_bias)
    return _proj_ln(xt, o.reshape(b * s, d), w_o.astype(BF16), ln_g, ln_b).reshape(b, s, d)


def _softplus(u):
    return jnp.maximum(u, 0.0) + jnp.log1p(jnp.exp(-jnp.abs(u)))


def _rwkv_proj_kernel(*refs, seq_blocks, has_vres):
    if has_vres:
        (x_ref, xp_ref, vf_ref, mix_ref, wr_ref, wk_ref, wv_ref, w0_ref, w1_ref, w2_ref, a0_ref, a1_ref, a2_ref,
         g1_ref, g2_ref, v0_ref, v1_ref, v2_ref, r_ref, k_ref, v_ref, lw_ref, a_ref, g_ref) = refs
    else:
        (x_ref, xp_ref, mix_ref, wr_ref, wk_ref, wv_ref, w0_ref, w1_ref, w2_ref, a0_ref, a1_ref, a2_ref,
         g1_ref, g2_ref, r_ref, k_ref, v_ref, lw_ref, a_ref, g_ref) = refs
    x = x_ref[...]
    rows = lax.broadcasted_iota(jnp.int32, x.shape, 0)
    first_in_seq = pl.program_id(0) % seq_blocks == 0
    carry_row = jnp.where(first_in_seq, 0.0, xp_ref[7:8, :])
    x_prev = jnp.where(rows == 0, carry_row, pltpu.roll(x, 1, axis=0))
    xx = x_prev - x

    def mixed(i):
        return (x + xx * mix_ref[i:i + 1, :]).astype(BF16)

    def lora(inp, w_in, w_out, act=None):
        mid = _dot(inp, w_in[...])
        if act is not None:
            mid = act(mid)
        return _dot(mid.astype(BF16), w_out[...])

    r_ref[...] = _dot(mixed(0), wr_ref[...])
    k_ref[...] = _dot(mixed(1), wk_ref[...])
    xv = mixed(2)
    v = _dot(xv, wv_ref[...])
    if has_vres:
        v = v + (vf_ref[...] - v) * jax.nn.sigmoid(v0_ref[...] + lora(xv, v1_ref, v2_ref))
    v_ref[...] = v
    w_log = -_softplus(-(w0_ref[...] + lora(mixed(3), w1_ref, w2_ref, jnp.tanh))) - 0.5
    lw_ref[...] = -jnp.exp(w_log)
    a_ref[...] = jax.nn.sigmoid(a0_ref[...] + lora(mixed(4), a1_ref, a2_ref))
    g_ref[...] = lora(mixed(5), g1_ref, g2_ref, jax.nn.sigmoid)


def _rwkv_proj(xt, seq_len, v_first, mix, w_rkv, w0, w1, w2, a0, a1, a2, g1, g2, vres):
    t, d = xt.shape
    assert seq_len % ROW_BLOCK == 0
    row = lambda i: (i, 0)
    fix = lambda i: (0, 0)
    prev8 = lambda i: (jnp.maximum(i * (ROW_BLOCK // 8) - 1, 0), 0)
    has_vres = vres is not None
    vec = lambda z: z.reshape(1, d)
    bf = lambda z: z.astype(BF16)
    ins = [xt, xt] + ([v_first] if has_vres else []) + [
        mix, bf(w_rkv[0]), bf(w_rkv[1]), bf(w_rkv[2]), vec(w0), bf(w1), bf(w2), vec(a0), bf(a1), bf(a2), bf(g1), bf(g2)]
    if has_vres:
        ins += [vec(vres[0]), bf(vres[1]), bf(vres[2])]
    specs = [pl.BlockSpec((ROW_BLOCK, d), row), pl.BlockSpec((8, d), prev8)]
    if has_vres:
        specs.append(pl.BlockSpec((ROW_BLOCK, d), row))
    specs += [pl.BlockSpec(z.shape, fix) for z in ins[len(specs):]]
    return pl.pallas_call(
        functools.partial(_rwkv_proj_kernel, seq_blocks=seq_len // ROW_BLOCK, has_vres=has_vres),
        grid=(t // ROW_BLOCK,),
        in_specs=specs,
        out_specs=[pl.BlockSpec((ROW_BLOCK, d), row)] * 6,
        out_shape=[jax.ShapeDtypeStruct((t, d), F32)] * 6,
        compiler_params=_params("parallel"),
    )(*ins)


def _wkv_kernel(r_ref, k_ref, v_ref, lw_ref, a_ref, kk_ref, ka_ref, rk_ref, gg_ref, gb_ref, o_ref, s_ref, *, n_chunks):
    c = WKV_CHUNK

    @pl.when(pl.program_id(2) == 0)
    def _reset():
        s_ref[...] = jnp.zeros_like(s_ref)

    lane = lax.broadcasted_iota(jnp.int32, (c, LANES), 1)
    row = lax.broadcasted_iota(jnp.int32, (c, LANES), 0)
    col = jnp.bitwise_and(lane, HEAD_DIM - 1)
    h0 = lane < HEAD_DIM
    sq_r = lax.broadcasted_iota(jnp.int32, (c, c), 0)
    sq_c = lax.broadcasted_iota(jnp.int32, (c, c), 1)
    tri_incl = jnp.where(sq_c <= sq_r, 1.0, 0.0).astype(BF16)
    eye = jnp.where(sq_c == sq_r, 1.0, 0.0)
    st_r = lax.broadcasted_iota(jnp.int32, (LANES, LANES), 0)
    st_c = lax.broadcasted_iota(jnp.int32, (LANES, LANES), 1)
    same_head = (st_r < HEAD_DIM) == (st_c < HEAD_DIM)
    k_k, k_a, r_k, gn_g, gn_b = kk_ref[...], ka_ref[...], rk_ref[...], gg_ref[...], gb_ref[...]

    def head_sum(z):
        s0 = jnp.sum(jnp.where(h0, z, 0.0), axis=-1, keepdims=True)
        s1 = jnp.sum(jnp.where(h0, 0.0, z), axis=-1, keepdims=True)
        return jnp.where(h0, s0, s1)

    for ci in range(n_chunks):
        rs = slice(ci * c, (ci + 1) * c)
        r, k, v, lw, a = r_ref[rs, :], k_ref[rs, :], v_ref[rs, :], lw_ref[rs, :], a_ref[rs, :]
        kk = k * k_k
        kk = kk / jnp.maximum(jnp.sqrt(head_sum(kk * kk)), 1e-12)
        kh = k * (1.0 + (a - 1.0) * k_a)
        aa = -kk
        bb = kk * a
        cum = sum(_dot(tri_incl, part) for part in _split3(lw))
        l_end = cum[c - 1:c, :]
        e_end = jnp.exp(l_end - cum)
        e_neg = jnp.exp(-cum)
        at = aa * jnp.exp(cum - lw)
        rt = r * jnp.exp(cum)
        bt_kt = jnp.concatenate([bb * e_neg, kh * e_neg], axis=0).astype(BF16)
        bh_kh = jnp.concatenate([bb * e_end, kh * e_end], axis=0).astype(BF16)
        vv = jnp.concatenate([v, v], axis=0).astype(BF16)
        at2 = jnp.zeros((c, LANES), F32)
        uv = jnp.zeros((c, LANES), F32)
        a_bot = []
        for own in (h0, jnp.logical_not(h0)):
            at_own = jnp.where(own, at, 0.0)
            lhs = jnp.concatenate([at_own, jnp.where(own, rt, 0.0)], axis=0).astype(BF16)
            p = _dot_t(lhs, bt_kt)
            a_top = jnp.where(col < row, p[:c, :], 0.0)
            a_bot.append(jnp.where(col <= row, p[c:, :], 0.0).astype(BF16))
            a_ab = a_top[:, :c]
            inv = eye + a_ab
            pw = a_ab
            span = 2
            while span < c:
                pw_b = pw.astype(BF16)
                pw = _dot(pw_b, pw_b)
                inv = inv + _dot(pw.astype(BF16), inv.astype(BF16))
                span *= 2
            inv_b = inv.astype(BF16)
            akv = _dot(jnp.where(h0, 0.0, a_top).astype(BF16), vv)
            at2 = at2 + _dot(inv_b, at_own.astype(BF16))
            uv = uv + jnp.where(own, _dot(inv_b, akv.astype(BF16)), 0.0)
        state = s_ref[...]
        state_b = state.astype(BF16)
        u = _dot_t(at2.astype(BF16), state_b) + uv
        u_v = jnp.concatenate([u, v], axis=0).astype(BF16)
        y = _dot_t(rt.astype(BF16), state_b) + jnp.where(h0, _dot(a_bot[0], u_v), _dot(a_bot[1], u_v))
        upd = lax.dot_general(u_v, bh_kh, _CONTRACT_FIRST, preferred_element_type=F32)
        s_ref[...] = state * jnp.exp(l_end) + jnp.where(same_head, upd, 0.0)
        mu = head_sum(y) * (1.0 / HEAD_DIM)
        yc = y - mu
        var = head_sum(yc * yc) * (1.0 / HEAD_DIM)
        yn = yc * lax.rsqrt(var + GN_EPS) * gn_g + gn_b
        o_ref[rs, :] = yn + head_sum(r * kh * r_k) * v


WKV_BLOCK = 512


def _wkv(r, k, v, lw, a, k_k, k_a, r_k, gn_g, gn_b, batch, seq_len):
    t, d = r.shape
    assert seq_len % WKV_BLOCK == 0 and WKV_BLOCK % WKV_CHUNK == 0
    nt = seq_len // WKV_BLOCK
    tok = lambda bi, hp, ti: (bi * nt + ti, hp)
    par = lambda bi, hp, ti: (0, hp)
    vec = lambda z: z.reshape(1, d)
    return pl.pallas_call(
        functools.partial(_wkv_kernel, n_chunks=WKV_BLOCK // WKV_CHUNK),
        grid=(batch, N_PAIRS, nt),
        in_specs=[pl.BlockSpec((WKV_BLOCK, LANES), tok)] * 5 + [pl.BlockSpec((1, LANES), par)] * 5,
        out_specs=pl.BlockSpec((WKV_BLOCK, LANES), tok),
        out_shape=jax.ShapeDtypeStruct((t, d), F32),
        scratch_shapes=[pltpu.VMEM((LANES, LANES), F32)],
        compiler_params=_params("parallel", "parallel", "arbitrary"),
    )(r, k, v, lw, a, vec(k_k), vec(k_a), vec(r_k), vec(gn_g), vec(gn_b))


def _rwkv_layer(x, v_first, mix, w_rkv, w0, w1, w2, a0, a1, a2, g1, g2, k_k, k_a, r_k, gn_g, gn_b, w_o, vres,
                ln_g, ln_b):
    b, s, d = x.shape
    xt = x.reshape(b * s, d)
    r, k, v, lw, a, g = _rwkv_proj(xt, s, v_first, mix, w_rkv, w0, w1, w2, a0, a1, a2, g1, g2, vres)
    y = _wkv(r, k, v, lw, a, k_k, k_a, r_k, gn_g, gn_b, b, s)
    out = _proj_ln(xt, y, w_o.astype(BF16), ln_g, ln_b, mult=g)
    return out.reshape(b, s, d), (v if vres is None else v_first)


def _router_kernel(x_ref, w_ref, o_ref):
    xh, xm, _ = _split3(x_ref[...])
    w = w_ref[...]
    wh = w.astype(BF16)
    wm = (w - wh.astype(F32)).astype(BF16)
    o_ref[...] = jax.nn.sigmoid(_dot(xh, wh) + _dot(xh, wm) + _dot(xm, wh))


def _router_scores(xt, w_router):
    t, d = xt.shape
    return pl.pallas_call(
        _router_kernel,
        grid=(t // ROW_BLOCK,),
        in_specs=[pl.BlockSpec((ROW_BLOCK, d), lambda i: (i, 0)), pl.BlockSpec((d, N_EXPERTS), lambda i: (0, 0))],
        out_specs=pl.BlockSpec((ROW_BLOCK, N_EXPERTS), lambda i: (i, 0)),
        out_shape=jax.ShapeDtypeStruct((t, N_EXPERTS), F32),
        compiler_params=_params("parallel"),
    )(xt, w_router.astype(F32))


def _route(scores, router_bias):
    biased = scores + router_bias.astype(F32)
    per_group = biased.reshape(-1, N_GROUPS, N_EXPERTS // N_GROUPS)
    group_score = jnp.sum(lax.top_k(per_group, 2)[0], -1)
    _, top_g = lax.top_k(group_score, TOPK_GROUPS)
    group_keep = jnp.any(top_g[:, :, None] == jnp.arange(N_GROUPS)[None, None, :], axis=1)
    expert_keep = jnp.repeat(group_keep, N_EXPERTS // N_GROUPS, axis=1)
    _, top_e = lax.top_k(jnp.where(expert_keep, biased, -jnp.inf), TOP_K)
    gate = jnp.take_along_axis(scores, top_e, axis=1)
    gate = gate / jnp.sum(gate, -1, keepdims=True) * ROUTED_SCALE
    return top_e, gate


def _group_rows(group_ids, n_groups, block):
    n = group_ids.shape[0]
    n_blocks = -(-n // block) + n_groups
    n_rows = n_blocks * block
    order = jnp.argsort(group_ids).astype(jnp.int32)
    g_sorted = group_ids[order]
    counts = jnp.bincount(group_ids, length=n_groups + 1)[:n_groups]
    padded = (counts + block - 1) // block * block
    pend = jnp.cumsum(padded)
    pstart = pend - padded
    sstart = jnp.cumsum(counts) - counts
    gc = jnp.minimum(g_sorted, n_groups - 1)
    row = jnp.where(g_sorted < n_groups, pstart[gc] + jnp.arange(n) - sstart[gc], n_rows)
    row_item = jnp.full((n_rows,), n, jnp.int32).at[row].set(order, mode='drop')
    block_group = jnp.minimum(jnp.searchsorted(pend, jnp.arange(n_blocks) * block, side='right'), n_groups - 1)
    return row_item, block_group.astype(jnp.int32)


def _expert_ffn_kernel(be_ref, x_ref, wg_ref, wu_ref, wd_ref, o_ref):
    del be_ref
    x = x_ref[...]
    h = jax.nn.silu(_dot(x, wg_ref[0])) * _dot(x, wu_ref[0])
    o_ref[...] = _dot(h.astype(BF16), wd_ref[0])


def _expert_ffn(x_rows, block_expert, w_gate, w_up, w_down):
    n_rows, d = x_rows.shape
    grid_spec = pltpu.PrefetchScalarGridSpec(
        num_scalar_prefetch=1,
        grid=(n_rows // MOE_ROWS,),
        in_specs=[pl.BlockSpec((MOE_ROWS, d), lambda i, be: (i, 0)),
                  pl.BlockSpec((1, d, D_EXPERT), lambda i, be: (be[i], 0, 0)),
                  pl.BlockSpec((1, d, D_EXPERT), lambda i, be: (be[i], 0, 0)),
                  pl.BlockSpec((1, D_EXPERT, d), lambda i, be: (be[i], 0, 0))],
        out_specs=pl.BlockSpec((MOE_ROWS, d), lambda i, be: (i, 0)),
    )
    return pl.pallas_call(
        _expert_ffn_kernel,
        grid_spec=grid_spec,
        out_shape=jax.ShapeDtypeStruct((n_rows, d), F32),
        compiler_params=_params("arbitrary"),
    )(block_expert, x_rows, w_gate, w_up, w_down)


def _shared_ln_kernel(x_ref, r_ref, sg_ref, su_ref, sd_ref, g_ref, b_ref, o_ref):
    x = x_ref[...]
    xb = x.astype(BF16)
    h = jax.nn.silu(_dot(xb, sg_ref[...])) * _dot(xb, su_ref[...])
    f = r_ref[...] + _dot(h.astype(BF16), sd_ref[...])
    o_ref[...] = _layer_norm_rows(ALPHA * x + f, g_ref[...], b_ref[...])


def _shared_ln(xt, routed, sh_gate, sh_up, sh_down, g, b):
    t, d = xt.shape
    row = lambda i: (i, 0)
    fix = lambda i: (0, 0)
    return pl.pallas_call(
        _shared_ln_kernel,
        grid=(t // ROW_BLOCK,),
        in_specs=[pl.BlockSpec((ROW_BLOCK, d), row), pl.BlockSpec((ROW_BLOCK, d), row),
                  pl.BlockSpec((d, D_EXPERT), fix), pl.BlockSpec((d, D_EXPERT), fix), pl.BlockSpec((D_EXPERT, d), fix),
                  pl.BlockSpec((1, d), fix), pl.BlockSpec((1, d), fix)],
        out_specs=pl.BlockSpec((ROW_BLOCK, d), row),
        out_shape=jax.ShapeDtypeStruct((t, d), F32),
        compiler_params=_params("parallel"),
    )(xt, routed, sh_gate.astype(BF16), sh_up.astype(BF16), sh_down.astype(BF16), g.reshape(1, d), b.reshape(1, d))


def _moe_layer(x, w_router, router_bias, w_gate, w_up, w_down, sh_gate, sh_up, sh_down, ln_g, ln_b):
    b, s, d = x.shape
    t = b * s
    xt = x.reshape(t, d)
    top_e, gate = _route(_router_scores(xt, w_router), router_bias)
    n_assign = t * TOP_K
    row_item, block_expert = _group_rows(top_e.reshape(n_assign), N_EXPERTS, MOE_ROWS)
    valid = row_item < n_assign
    row_tok = jnp.where(valid, row_item // TOP_K, t)
    row_gate = jnp.where(valid, gate.reshape(n_assign)[jnp.minimum(row_item, n_assign - 1)], 0.0)
    x_pad = jnp.concatenate([xt.astype(BF16), jnp.zeros((1, d), BF16)], 0)
    y_rows = _expert_ffn(x_pad[row_tok], block_expert, w_gate.astype(BF16), w_up.astype(BF16), w_down.astype(BF16))
    routed = jnp.zeros((t + 1, d), F32).at[row_tok].add(y_rows * row_gate[:, None])[:t]
    return _shared_ln(xt, routed, sh_gate, sh_up, sh_down, ln_g, ln_b).reshape(b, s, d)


def kernel(x, moba_w_qkv, moba_w_o, rel_bias, rwkv_mix, rwkv_w_rkv, rwkv_w0, rwkv_w1, rwkv_w2,
           rwkv_a0, rwkv_a1, rwkv_a2, rwkv_v0, rwkv_v1, rwkv_v2, rwkv_g1, rwkv_g2, rwkv_k_k,
           rwkv_k_a, rwkv_r_k, rwkv_gn_g, rwkv_gn_b, rwkv_w_o, moe_w_router, moe_router_bias,
           moe_w_gate, moe_w_up, moe_w_down, moe_sh_gate, moe_sh_up, moe_sh_down,
           ln_mix_g, ln_mix_b, ln_ffn_g, ln_ffn_b):
    assert x.shape[-1] == D_MODEL
    own_bias, prev_bias = _moba_bias_tables(rel_bias)
    v_first = None
    for i in range(DEPTH):
        j = i // 2
        if i % 2 == 0:
            x = _moba_layer(x, moba_w_qkv[j], moba_w_o[j], own_bias, prev_bias, ln_mix_g[i], ln_mix_b[i])
        else:
            vres = None if j == 0 else (rwkv_v0[j - 1], rwkv_v1[j - 1], rwkv_v2[j - 1])
            x, v_first = _rwkv_layer(x, v_first, rwkv_mix[j], rwkv_w_rkv[j], rwkv_w0[j], rwkv_w1[j],
                                     rwkv_w2[j], rwkv_a0[j], rwkv_a1[j], rwkv_a2[j], rwkv_g1[j],
                                     rwkv_g2[j], rwkv_k_k[j], rwkv_k_a[j], rwkv_r_k[j],
                                     rwkv_gn_g[j], rwkv_gn_b[j], rwkv_w_o[j], vres, ln_mix_g[i], ln_mix_b[i])
        x = _moe_layer(x, moe_w_router[i], moe_router_bias[i], moe_w_gate[i], moe_w_up[i], moe_w_down[i],
                       moe_sh_gate[i], moe_sh_up[i], moe_sh_down[i], ln_ffn_g[i], ln_ffn_b[i])
    return x
```

```python
import functools
import math

import jax
import jax.numpy as jnp
import numpy as np
from jax import lax
from jax.experimental import pallas as pl
from jax.experimental.pallas import tpu as pltpu

F32 = jnp.float32
BF16 = jnp.bfloat16

D_MODEL = 1024
DEPTH = 4
ALPHA = (2 * DEPTH) ** 0.25
LN_EPS = 1e-5

HEAD_DIM = 64
N_HEADS = D_MODEL // HEAD_DIM
LANES = 128
N_PAIRS = D_MODEL // LANES
MOBA_BLOCK = 256
MOBA_TOPK = 3
N_BUCKETS = 32
MAX_DISTANCE = 128
MASKED = -1e30

GN_EPS = 64e-5
WKV_CHUNK = 64

N_EXPERTS = 64
N_GROUPS = 8
TOPK_GROUPS = 4
TOP_K = 8
D_EXPERT = 256
ROUTED_SCALE = 2.5
MOE_ROWS = 512

VMEM_LIMIT = 56 * 1024 * 1024
ROW_BLOCK = 512

_CONTRACT_LAST = (((1,), (1,)), ((), ()))
_CONTRACT_FIRST = (((0,), (0,)), ((), ()))


def _params(*sem):
    return pltpu.CompilerParams(dimension_semantics=sem, vmem_limit_bytes=VMEM_LIMIT)


def _dot(a, b):
    return jnp.dot(a, b, preferred_element_type=F32)


def _dot_t(a, b):
    return lax.dot_general(a, b, _CONTRACT_LAST, preferred_element_type=F32)


def _split3(x):
    hi = x.astype(BF16)
    r1 = x - hi.astype(F32)
    mid = r1.astype(BF16)
    lo = (r1 - mid.astype(F32)).astype(BF16)
    return hi, mid, lo


def _layer_norm_rows(y, g, b):
    mu = jnp.mean(y, axis=-1, keepdims=True)
    yc = y - mu
    var = jnp.mean(yc * yc, axis=-1, keepdims=True)
    return yc * lax.rsqrt(var + LN_EPS) * g + b


def _mm_kernel(x_ref, w_ref, o_ref):
    x = x_ref[...].astype(BF16)
    o_ref[...] = _dot(x, w_ref[...]).astype(o_ref.dtype)


def _matmul(x, w, out_dtype):
    m, k = x.shape
    n = w.shape[1]
    return pl.pallas_call(
        _mm_kernel,
        grid=(m // ROW_BLOCK,),
        in_specs=[pl.BlockSpec((ROW_BLOCK, k), lambda i: (i, 0)),
                  pl.BlockSpec((k, n), lambda i: (0, 0))],
        out_specs=pl.BlockSpec((ROW_BLOCK, n), lambda i: (i, 0)),
        out_shape=jax.ShapeDtypeStruct((m, n), out_dtype),
        compiler_params=_params("parallel"),
    )(x, w)


def _proj_ln_kernel(x_ref, a_ref, w_ref, g_ref, b_ref, o_ref):
    h = _dot(a_ref[...].astype(BF16), w_ref[...])
    o_ref[...] = _layer_norm_rows(ALPHA * x_ref[...] + h, g_ref[...], b_ref[...])


def _proj_gate_ln_kernel(x_ref, a_ref, m_ref, w_ref, g_ref, b_ref, o_ref):
    h = _dot((a_ref[...] * m_ref[...]).astype(BF16), w_ref[...])
    o_ref[...] = _layer_norm_rows(ALPHA * x_ref[...] + h, g_ref[...], b_ref[...])


def _proj_ln(x, a, w, g, b, mult=None):
    t, d = x.shape
    k = a.shape[1]
    row = lambda i: (i, 0)
    fix = lambda i: (0, 0)
    acts = [a] if mult is None else [a, mult]
    kern = _proj_ln_kernel if mult is None else _proj_gate_ln_kernel
    return pl.pallas_call(
        kern,
        grid=(t // ROW_BLOCK,),
        in_specs=[pl.BlockSpec((ROW_BLOCK, d), row)]
        + [pl.BlockSpec((ROW_BLOCK, k), row) for _ in acts]
        + [pl.BlockSpec((k, d), fix), pl.BlockSpec((1, d), fix), pl.BlockSpec((1, d), fix)],
        out_specs=pl.BlockSpec((ROW_BLOCK, d), row),
        out_shape=jax.ShapeDtypeStruct((t, d), F32),
        compiler_params=_params("parallel"),
    )(x, *acts, w, g.reshape(1, d), b.reshape(1, d))


def _rel_bucket_np(dist):
    n = np.maximum(dist, 0)
    max_exact = N_BUCKETS // 2
    nf = np.maximum(n, 1).astype(np.float32)
    large = max_exact + (np.log(nf / np.float32(max_exact)) / np.float32(math.log(MAX_DISTANCE / max_exact))
                         * np.float32(N_BUCKETS - max_exact)).astype(np.int32)
    return np.where(n < max_exact, n, np.minimum(large, N_BUCKETS - 1))


def _moba_bias_tables(rel_bias):
    off = np.arange(MOBA_BLOCK)
    rel = off[:, None] - off[None, :]
    assert int(_rel_bucket_np(np.array([MOBA_BLOCK + 1]))[0]) == N_BUCKETS - 1
    bias_hd = rel_bias.astype(F32).T
    far = bias_hd[:, N_BUCKETS - 1][:, None, None]
    own = bias_hd[:, _rel_bucket_np(rel)] - far
    own = jnp.where(jnp.asarray(rel >= 0)[None], own, MASKED)
    prev = bias_hd[:, _rel_bucket_np(rel + MOBA_BLOCK)] - far
    return own, prev


def _moba_kernel(q_ref, k_ref, v_ref, own_ref, prev_ref, o_ref, km_ref, m_ref, acc_ref, *, nblk):
    qb = pl.program_id(2)
    blk = MOBA_BLOCK
    lane = lax.broadcasted_iota(jnp.int32, (blk, LANES), 1)
    lane_sq = lax.broadcasted_iota(jnp.int32, (LANES, LANES), 1)

    @pl.when(qb == 0)
    def _block_means():
        rows = [jnp.sum(k_ref[0, j * blk:(j + 1) * blk, :].astype(F32), axis=0, keepdims=True) * (1.0 / blk)
                for j in range(nblk)]
        km = jnp.concatenate(rows, axis=0)
        lane_k = lax.broadcasted_iota(jnp.int32, (nblk, LANES), 1)
        km0 = jnp.where(lane_k < HEAD_DIM, km, 0.0)
        km1 = jnp.where(lane_k >= HEAD_DIM, km, 0.0)
        z = lambda r: jnp.zeros((r, LANES), F32)
        km_ref[0] = jnp.concatenate([z(HEAD_DIM), km0, z(HEAD_DIM - nblk)], axis=0)
        km_ref[1] = jnp.concatenate([km1, z(LANES - nblk)], axis=0)

    q = q_ref[0]
    heads = []
    for h in range(2):
        own = (lane < HEAD_DIM) if h == 0 else (lane >= HEAD_DIM)
        goff = HEAD_DIM if h == 0 else 0
        kmf = km_ref[h]
        kmh = kmf.astype(BF16)
        kml = (kmf - kmh.astype(F32)).astype(BF16)
        gate = _dot_t(q, kmh) + _dot_t(q, kml)
        jblk = lane - goff
        valid = (jblk >= 0) & (jblk < qb)
        gm = jnp.where(valid, gate, -jnp.inf)
        sel = jnp.zeros((blk, LANES), jnp.bool_)
        for _ in range(MOBA_TOPK):
            mx = jnp.max(gm, axis=-1, keepdims=True)
            first = jnp.min(jnp.where(gm == mx, lane, 1 << 20), axis=-1, keepdims=True)
            pick = (lane == first) & (mx > -jnp.inf)
            sel = sel | pick
            gm = jnp.where(pick, -jnp.inf, gm)
        gsel = jnp.where(valid & jnp.logical_not(sel), MASKED, 0.0)
        q_aug = jnp.where(own, q, gsel.astype(BF16))
        q_own = jnp.where(own, q, jnp.zeros_like(q))
        heads.append((own, goff, q_aug, q_own))

    def attend(h, s, v_aug, first):
        m_cur = jnp.max(s, axis=-1, keepdims=True)
        if first:
            m_new = m_cur
            p = jnp.exp(s - m_new)
            acc_ref[h] = _dot(p.astype(BF16), v_aug)
        else:
            m_prev = m_ref[h][:, :1]
            m_new = jnp.maximum(m_prev, m_cur)
            p = jnp.exp(s - m_new)
            acc_ref[h] = acc_ref[h] * jnp.exp(m_prev - m_new) + _dot(p.astype(BF16), v_aug)
        m_ref[h] = jnp.broadcast_to(m_new, (blk, LANES))

    def kv_block(j):
        start = pl.multiple_of(j * blk, blk)
        return k_ref[0, pl.ds(start, blk), :], v_ref[0, pl.ds(start, blk), :]

    def v_with_ones(own, vj):
        return jnp.where(own, vj, jnp.ones_like(vj))

    kj, vj = kv_block(qb)
    for h, (own, goff, q_aug, q_own) in enumerate(heads):
        attend(h, _dot_t(q_own, kj) + own_ref[h], v_with_ones(own, vj), True)

    def past(j, with_prev_bias):
        kj, vj = kv_block(j)
        for h, (own, goff, q_aug, q_own) in enumerate(heads):
            ej = jnp.where(lane == goff + j, 1.0, 0.0).astype(BF16)
            s = _dot_t(q_aug, jnp.where(own, kj, ej))
            if with_prev_bias:
                s = s + prev_ref[h]
            attend(h, s, v_with_ones(own, vj), False)

    @pl.when(qb >= 1)
    def _prev_block():
        past(qb - 1, True)

    def far_body(j, carry):
        past(j, False)
        return carry

    lax.fori_loop(0, jnp.maximum(qb - 1, 0), far_body, 0)

    a0 = acc_ref[0]
    a1 = acc_ref[1]
    o0 = a0 / pltpu.roll(a0, HEAD_DIM, axis=1)
    o1 = a1 / pltpu.roll(a1, HEAD_DIM, axis=1)
    o_ref[0] = jnp.where(lane < HEAD_DIM, o0, o1).astype(o_ref.dtype)


def _moba_attention(qkv, own_bias, prev_bias):
    b, s, _ = qkv.shape
    assert s % MOBA_BLOCK == 0
    nblk = s // MOBA_BLOCK
    assert nblk <= HEAD_DIM
    return pl.pallas_call(
        functools.partial(_moba_kernel, nblk=nblk),
        grid=(b, N_PAIRS, nblk),
        in_specs=[
            pl.BlockSpec((1, MOBA_BLOCK, LANES), lambda bi, hp, qb: (bi, qb, hp)),
            pl.BlockSpec((1, s, LANES), lambda bi, hp, qb: (bi, 0, N_PAIRS + hp)),
            pl.BlockSpec((1, s, LANES), lambda bi, hp, qb: (bi, 0, 2 * N_PAIRS + hp)),
            pl.BlockSpec((2, MOBA_BLOCK, MOBA_BLOCK), lambda bi, hp, qb: (hp, 0, 0)),
            pl.BlockSpec((2, MOBA_BLOCK, MOBA_BLOCK), lambda bi, hp, qb: (hp, 0, 0)),
        ],
        out_specs=pl.BlockSpec((1, MOBA_BLOCK, LANES), lambda bi, hp, qb: (bi, qb, hp)),
        out_shape=jax.ShapeDtypeStruct((b, s, D_MODEL), BF16),
        scratch_shapes=[pltpu.VMEM((2, LANES, LANES), F32),
                        pltpu.VMEM((2, MOBA_BLOCK, LANES), F32),
                        pltpu.VMEM((2, MOBA_BLOCK, LANES), F32)],
        compiler_params=_params("parallel", "parallel", "arbitrary"),
    )(qkv, qkv, qkv, own_bias, prev_bias)


def _moba_layer(x, w_qkv, w_o, own_bias, prev_bias, ln_g, ln_b):
    b, s, d = x.shape
    xt = x.reshape(b * s, d)
    scale = jnp.concatenate([jnp.full((d,), HEAD_DIM ** -0.5, F32), jnp.ones((2 * d,), F32)])
    qkv = _matmul(xt, (w_qkv * scale).astype(BF16), BF16)
    o = _moba_attention(qkv.reshape(b, s, 3 * d), own_bias, prev_bias)
    return _proj_ln(xt, o.reshape(b * s, d), w_o.astype(BF16), ln_g, ln_b).reshape(b, s, d)


def _softplus(u):
    return jnp.maximum(u, 0.0) + jnp.log1p(jnp.exp(-jnp.abs(u)))


def _rwkv_proj_kernel(*refs, seq_blocks, has_vres):
    if has_vres:
        (x_ref, xp_ref, vf_ref, mix_ref, wr_ref, wk_ref, wv_ref, w0_ref, w1_ref, w2_ref, a0_ref, a1_ref, a2_ref,
         g1_ref, g2_ref, v0_ref, v1_ref, v2_ref, r_ref, k_ref, v_ref, lw_ref, a_ref, g_ref) = refs
    else:
        (x_ref, xp_ref, mix_ref, wr_ref, wk_ref, wv_ref, w0_ref, w1_ref, w2_ref, a0_ref, a1_ref, a2_ref,
         g1_ref, g2_ref, r_ref, k_ref, v_ref, lw_ref, a_ref, g_ref) = refs
    x = x_ref[...]
    rows = lax.broadcasted_iota(jnp.int32, x.shape, 0)
    first_in_seq = pl.program_id(0) % seq_blocks == 0
    carry_row = jnp.where(first_in_seq, 0.0, xp_ref[7:8, :])
    x_prev = jnp.where(rows == 0, carry_row, pltpu.roll(x, 1, axis=0))
    xx = x_prev - x

    def mixed(i):
        return (x + xx * mix_ref[i:i + 1, :]).astype(BF16)

    def lora(inp, w_in, w_out, act=None):
        mid = _dot(inp, w_in[...])
        if act is not None:
            mid = act(mid)
        return _dot(mid.astype(BF16), w_out[...])

    r_ref[...] = _dot(mixed(0), wr_ref[...])
    k_ref[...] = _dot(mixed(1), wk_ref[...])
    xv = mixed(2)
    v = _dot(xv, wv_ref[...])
    if has_vres:
        v = v + (vf_ref[...] - v) * jax.nn.sigmoid(v0_ref[...] + lora(xv, v1_ref, v2_ref))
    v_ref[...] = v
    w_log = -_softplus(-(w0_ref[...] + lora(mixed(3), w1_ref, w2_ref, jnp.tanh))) - 0.5
    lw_ref[...] = -jnp.exp(w_log)
    a_ref[...] = jax.nn.sigmoid(a0_ref[...] + lora(mixed(4), a1_ref, a2_ref))
    g_ref[...] = lora(mixed(5), g1_ref, g2_ref, jax.nn.sigmoid)


def _rwkv_proj(xt, seq_len, v_first, mix, w_rkv, w0, w1, w2, a0, a1, a2, g1, g2, vres):
    t, d = xt.shape
    assert seq_len % ROW_BLOCK == 0
    row = lambda i: (i, 0)
    fix = lambda i: (0, 0)
    prev8 = lambda i: (jnp.maximum(i * (ROW_BLOCK // 8) - 1, 0), 0)
    has_vres = vres is not None
    vec = lambda z: z.reshape(1, d)
    bf = lambda z: z.astype(BF16)
    ins = [xt, xt] + ([v_first] if has_vres else []) + [
        mix, bf(w_rkv[0]), bf(w_rkv[1]), bf(w_rkv[2]), vec(w0), bf(w1), bf(w2), vec(a0), bf(a1), bf(a2), bf(g1), bf(g2)]
    if has_vres:
        ins += [vec(vres[0]), bf(vres[1]), bf(vres[2])]
    specs = [pl.BlockSpec((ROW_BLOCK, d), row), pl.BlockSpec((8, d), prev8)]
    if has_vres:
        specs.append(pl.BlockSpec((ROW_BLOCK, d), row))
    specs += [pl.BlockSpec(z.shape, fix) for z in ins[len(specs):]]
    return pl.pallas_call(
        functools.partial(_rwkv_proj_kernel, seq_blocks=seq_len // ROW_BLOCK, has_vres=has_vres),
        grid=(t // ROW_BLOCK,),
        in_specs=specs,
        out_specs=[pl.BlockSpec((ROW_BLOCK, d), row)] * 6,
        out_shape=[jax.ShapeDtypeStruct((t, d), F32)] * 6,
        compiler_params=_params("parallel"),
    )(*ins)


def _wkv_kernel(r_ref, k_ref, v_ref, lw_ref, a_ref, kk_ref, ka_ref, rk_ref, gg_ref, gb_ref, o_ref, s_ref, *, n_chunks):
    c = WKV_CHUNK

    @pl.when(pl.program_id(2) == 0)
    def _reset():
        s_ref[...] = jnp.zeros_like(s_ref)

    lane = lax.broadcasted_iota(jnp.int32, (c, LANES), 1)
    row = lax.broadcasted_iota(jnp.int32, (c, LANES), 0)
    col = jnp.bitwise_and(lane, HEAD_DIM - 1)
    h0 = lane < HEAD_DIM
    sq_r = lax.broadcasted_iota(jnp.int32, (c, c), 0)
    sq_c = lax.broadcasted_iota(jnp.int32, (c, c), 1)
    tri_incl = jnp.where(sq_c <= sq_r, 1.0, 0.0).astype(BF16)
    eye = jnp.where(sq_c == sq_r, 1.0, 0.0)
    st_r = lax.broadcasted_iota(jnp.int32, (LANES, LANES), 0)
    st_c = lax.broadcasted_iota(jnp.int32, (LANES, LANES), 1)
    same_head = (st_r < HEAD_DIM) == (st_c < HEAD_DIM)
    k_k, k_a, r_k, gn_g, gn_b = kk_ref[...], ka_ref[...], rk_ref[...], gg_ref[...], gb_ref[...]

    def head_sum(z):
        s0 = jnp.sum(jnp.where(h0, z, 0.0), axis=-1, keepdims=True)
        s1 = jnp.sum(jnp.where(h0, 0.0, z), axis=-1, keepdims=True)
        return jnp.where(h0, s0, s1)

    for ci in range(n_chunks):
        rs = slice(ci * c, (ci + 1) * c)
        r, k, v, lw, a = r_ref[rs, :], k_ref[rs, :], v_ref[rs, :], lw_ref[rs, :], a_ref[rs, :]
        kk = k * k_k
        kk = kk / jnp.maximum(jnp.sqrt(head_sum(kk * kk)), 1e-12)
        kh = k * (1.0 + (a - 1.0) * k_a)
        aa = -kk
        bb = kk * a
        cum = sum(_dot(tri_incl, part) for part in _split3(lw))
        l_end = cum[c - 1:c, :]
        e_end = jnp.exp(l_end - cum)
        e_neg = jnp.exp(-cum)
        at = aa * jnp.exp(cum - lw)
        rt = r * jnp.exp(cum)
        bt_kt = jnp.concatenate([bb * e_neg, kh * e_neg], axis=0).astype(BF16)
        bh_kh = jnp.concatenate([bb * e_end, kh * e_end], axis=0).astype(BF16)
        vv = jnp.concatenate([v, v], axis=0).astype(BF16)
        at2 = jnp.zeros((c, LANES), F32)
        uv = jnp.zeros((c, LANES), F32)
        a_bot = []
        for own in (h0, jnp.logical_not(h0)):
            at_own = jnp.where(own, at, 0.0)
            lhs = jnp.concatenate([at_own, jnp.where(own, rt, 0.0)], axis=0).astype(BF16)
            p = _dot_t(lhs, bt_kt)
            a_top = jnp.where(col < row, p[:c, :], 0.0)
            a_bot.append(jnp.where(col <= row, p[c:, :], 0.0).astype(BF16))
            a_ab = a_top[:, :c]
            inv = eye + a_ab
            pw = a_ab
            span = 2
            while span < c:
                pw_b = pw.astype(BF16)
                pw = _dot(pw_b, pw_b)
                inv = inv + _dot(pw.astype(BF16), inv.astype(BF16))
                span *= 2
            inv_b = inv.astype(BF16)
            akv = _dot(jnp.where(h0, 0.0, a_top).astype(BF16), vv)
            at2 = at2 + _dot(inv_b, at_own.astype(BF16))
            uv = uv + jnp.where(own, _dot(inv_b, akv.astype(BF16)), 0.0)
        state = s_ref[...]
        state_b = state.astype(BF16)
        u = _dot_t(at2.astype(BF16), state_b) + uv
        u_v = jnp.concatenate([u, v], axis=0).astype(BF16)
        y = _dot_t(rt.astype(BF16), state_b) + jnp.where(h0, _dot(a_bot[0], u_v), _dot(a_bot[1], u_v))
        upd = lax.dot_general(u_v, bh_kh, _CONTRACT_FIRST, preferred_element_type=F32)
        s_ref[...] = state * jnp.exp(l_end) + jnp.where(same_head, upd, 0.0)
        mu = head_sum(y) * (1.0 / HEAD_DIM)
        yc = y - mu
        var = head_sum(yc * yc) * (1.0 / HEAD_DIM)
        yn = yc * lax.rsqrt(var + GN_EPS) * gn_g + gn_b
        o_ref[rs, :] = yn + head_sum(r * kh * r_k) * v


WKV_BLOCK = 512


def _wkv(r, k, v, lw, a, k_k, k_a, r_k, gn_g, gn_b, batch, seq_len):
    t, d = r.shape
    assert seq_len % WKV_BLOCK == 0 and WKV_BLOCK % WKV_CHUNK == 0
    nt = seq_len // WKV_BLOCK
    tok = lambda bi, hp, ti: (bi * nt + ti, hp)
    par = lambda bi, hp, ti: (0, hp)
    vec = lambda z: z.reshape(1, d)
    return pl.pallas_call(
        functools.partial(_wkv_kernel, n_chunks=WKV_BLOCK // WKV_CHUNK),
        grid=(batch, N_PAIRS, nt),
        in_specs=[pl.BlockSpec((WKV_BLOCK, LANES), tok)] * 5 + [pl.BlockSpec((1, LANES), par)] * 5,
        out_specs=pl.BlockSpec((WKV_BLOCK, LANES), tok),
        out_shape=jax.ShapeDtypeStruct((t, d), F32),
        scratch_shapes=[pltpu.VMEM((LANES, LANES), F32)],
        compiler_params=_params("parallel", "parallel", "arbitrary"),
    )(r, k, v, lw, a, vec(k_k), vec(k_a), vec(r_k), vec(gn_g), vec(gn_b))


def _rwkv_layer(x, v_first, mix, w_rkv, w0, w1, w2, a0, a1, a2, g1, g2, k_k, k_a, r_k, gn_g, gn_b, w_o, vres,
                ln_g, ln_b):
    b, s, d = x.shape
    xt = x.reshape(b * s, d)
    r, k, v, lw, a, g = _rwkv_proj(xt, s, v_first, mix, w_rkv, w0, w1, w2, a0, a1, a2, g1, g2, vres)
    y = _wkv(r, k, v, lw, a, k_k, k_a, r_k, gn_g, gn_b, b, s)
    out = _proj_ln(xt, y, w_o.astype(BF16), ln_g, ln_b, mult=g)
    return out.reshape(b, s, d), (v if vres is None else v_first)


def _first_max(vals, idx, big):
    mx = jnp.max(vals, axis=0, keepdims=True)
    return mx, jnp.min(jnp.where(vals == mx, idx, big), axis=0, keepdims=True)


def _route_kernel(x_ref, w_ref, b_ref, pos_ref, gate_ref, off_ref, cnt_ref, *, tt):
    xh, xm, _ = _split3(x_ref[...])
    w = w_ref[...]
    wh = w.astype(BF16)
    wm = (w - wh.astype(F32)).astype(BF16)
    scores = jax.nn.sigmoid(_dot_t(wh, xh) + _dot_t(wh, xm) + _dot_t(wm, xh))
    biased = scores + b_ref[...]
    per = N_EXPERTS // N_GROUPS
    neg = -jnp.inf
    row_g = lax.broadcasted_iota(jnp.int32, (per, tt), 0)
    group_scores = []
    for g in range(N_GROUPS):
        grp = biased[g * per:(g + 1) * per, :]
        m1, i1 = _first_max(grp, row_g, per)
        m2 = jnp.max(jnp.where(row_g == i1, neg, grp), axis=0, keepdims=True)
        group_scores.append(m1 + m2)
    gsc = jnp.concatenate(group_scores, axis=0)
    row_n = lax.broadcasted_iota(jnp.int32, (N_GROUPS, tt), 0)
    keep = jnp.zeros((N_GROUPS, tt), F32)
    for _ in range(TOPK_GROUPS):
        _, ix = _first_max(gsc, row_n, N_GROUPS)
        pick = row_n == ix
        keep = jnp.where(pick, 1.0, keep)
        gsc = jnp.where(pick, neg, gsc)
    keep_e = jnp.concatenate([jnp.broadcast_to(keep[g:g + 1, :], (per, tt)) for g in range(N_GROUPS)], axis=0)
    cand = jnp.where(keep_e > 0.0, biased, neg)
    row_e = lax.broadcasted_iota(jnp.int32, (N_EXPERTS, tt), 0)
    picks = []
    for _ in range(TOP_K):
        _, ix = _first_max(cand, row_e, N_EXPERTS)
        pick = row_e == ix
        picks.append(pick)
        cand = jnp.where(pick, neg, cand)
    raw = [jnp.sum(jnp.where(p, scores, 0.0), axis=0, keepdims=True) for p in picks]
    total = raw[0]
    for r in raw[1:]:
        total = total + r
    gate_ref[0] = jnp.concatenate([r / total * ROUTED_SCALE for r in raw], axis=0)

    sel = jnp.zeros((N_EXPERTS, tt), F32)
    for p in picks:
        sel = jnp.where(p, 1.0, sel)
    cb = 256
    cr = lax.broadcasted_iota(jnp.int32, (cb, cb), 0)
    cc = lax.broadcasted_iota(jnp.int32, (cb, cb), 1)
    upper_incl = jnp.where(cr <= cc, 1.0, 0.0).astype(BF16)
    carry = jnp.zeros((N_EXPERTS, 1), F32)
    ranks = []
    for c0 in range(0, tt, cb):
        blk = sel[:, c0:c0 + cb]
        incl = _dot(blk.astype(BF16), upper_incl)
        ranks.append(incl - blk + carry)
        carry = carry + incl[:, cb - 1:cb]
    rank = jnp.concatenate(ranks, axis=1)
    count = jnp.broadcast_to(carry, (N_EXPERTS, LANES))
    aligned8 = jnp.floor((count + 7.0) * 0.125)
    er = lax.broadcasted_iota(jnp.int32, (N_EXPERTS, N_EXPERTS), 0)
    ec = lax.broadcasted_iota(jnp.int32, (N_EXPERTS, N_EXPERTS), 1)
    strict_lower = jnp.where(ec < er, 1.0, 0.0).astype(BF16)
    start = 8.0 * _dot(strict_lower, aligned8.astype(BF16))
    where_to = start[:, :1] + rank
    pos_ref[0] = jnp.concatenate(
        [jnp.sum(jnp.where(p, where_to, 0.0), axis=0, keepdims=True) for p in picks], axis=0).astype(jnp.int32)
    off_ref[0] = start.astype(jnp.int32)
    cnt_ref[0] = count.astype(jnp.int32)


MOE_TILE = 1024
MOE_CHUNK = 64
SEG_ALIGN = 8


def _route_tiles(xt, w_router, router_bias):
    t, d = xt.shape
    tt = MOE_TILE
    n_tiles = t // tt
    tile3 = lambda i: (i, 0, 0)
    pos, gate, off, cnt = pl.pallas_call(
        functools.partial(_route_kernel, tt=tt),
        grid=(n_tiles,),
        in_specs=[pl.BlockSpec((tt, d), lambda i: (i, 0)),
                  pl.BlockSpec((N_EXPERTS, d), lambda i: (0, 0)),
                  pl.BlockSpec((N_EXPERTS, 1), lambda i: (0, 0))],
        out_specs=[pl.BlockSpec((1, TOP_K, tt), tile3), pl.BlockSpec((1, TOP_K, tt), tile3),
                   pl.BlockSpec((1, N_EXPERTS, LANES), tile3), pl.BlockSpec((1, N_EXPERTS, LANES), tile3)],
        out_shape=[jax.ShapeDtypeStruct((n_tiles, TOP_K, tt), jnp.int32),
                   jax.ShapeDtypeStruct((n_tiles, TOP_K, tt), F32),
                   jax.ShapeDtypeStruct((n_tiles, N_EXPERTS, LANES), jnp.int32),
                   jax.ShapeDtypeStruct((n_tiles, N_EXPERTS, LANES), jnp.int32)],
        compiler_params=_params("parallel"),
    )(xt, w_router.astype(F32).T, router_bias.astype(F32).reshape(N_EXPERTS, 1))
    return (pos.reshape(n_tiles, TOP_K * tt), gate.reshape(n_tiles, TOP_K * tt),
            off[:, :, 0].reshape(-1), cnt[:, :, 0].reshape(-1))


def _moe_tile_kernel(off_ref, cnt_ref, x_hbm, pos_hbm, gate_hbm, wg_ref, wu_ref, wd_ref, o_hbm,
                     stage, tok, pos_s, gate_s, sems, *, tt):
    ti = pl.program_id(0)
    e = pl.program_id(1)
    sub = D_MODEL // LANES
    rows = MOE_CHUNK

    def tile_copy(src, dst, sem):
        return pltpu.make_async_copy(src, dst, sem)

    @pl.when(e == 0)
    def _distribute():
        copies = [tile_copy(x_hbm.at[pl.ds(pl.multiple_of(ti * tt * sub, tt * sub), tt * sub)], tok, sems.at[0]),
                  tile_copy(pos_hbm.at[ti], pos_s, sems.at[1]),
                  tile_copy(gate_hbm.at[ti], gate_s, sems.at[2])]
        for c in copies:
            c.start()
        for c in copies:
            c.wait()
        zero_group = jnp.zeros((SEG_ALIGN * sub, LANES), F32)

        def clear_tail(ex, carry):
            end = off_ref[ti * N_EXPERTS + ex] + cnt_ref[ti * N_EXPERTS + ex]
            g0 = pl.multiple_of(lax.shift_left(lax.shift_right_logical(end, 3), 3) * sub, SEG_ALIGN * sub)
            stage[pl.ds(g0, SEG_ALIGN * sub), :] = zero_group
            return carry

        lax.fori_loop(0, N_EXPERTS, clear_tail, 0)
        last = ti * N_EXPERTS + N_EXPERTS - 1
        total = off_ref[last] + lax.shift_left(lax.shift_right_logical(cnt_ref[last] + 7, 3), 3)
        t0 = pl.multiple_of(total * sub, SEG_ALIGN * sub)
        stage[pl.ds(t0, rows * sub), :] = jnp.zeros((rows * sub, LANES), F32)

        def place(t, carry):
            row = tok[pl.ds(pl.multiple_of(t * sub, sub), sub), :]
            for j in range(TOP_K):
                p = pos_s[j * tt + t]
                stage[pl.ds(pl.multiple_of(p * sub, sub), sub), :] = row
            return carry

        lax.fori_loop(0, tt, place, 0)

    off = off_ref[ti * N_EXPERTS + e]
    cnt = cnt_ref[ti * N_EXPERTS + e]
    wg, wu, wd = wg_ref[0], wu_ref[0], wd_ref[0]
    row_id = lax.broadcasted_iota(jnp.int32, (rows, LANES), 0)

    def ffn_step(c, carry):
        base = pl.multiple_of((off + c * rows) * sub, SEG_ALIGN * sub)
        parts = [stage[pl.ds(base + k, rows, stride=sub), :] for k in range(sub)]
        xb = jnp.concatenate(parts, axis=1).astype(BF16)
        h = jax.nn.silu(_dot(xb, wg)) * _dot(xb, wu)
        y = _dot(h.astype(BF16), wd)
        live = row_id < cnt - c * rows
        for k in range(sub):
            stage[pl.ds(base + k, rows, stride=sub), :] = jnp.where(live, y[:, k * LANES:(k + 1) * LANES], parts[k])
        return carry

    lax.fori_loop(0, lax.shift_right_logical(cnt + rows - 1, rows.bit_length() - 1), ffn_step, 0)

    @pl.when(e == N_EXPERTS - 1)
    def _combine():
        def gather(t, carry):
            acc = jnp.zeros((sub, LANES), F32)
            for j in range(TOP_K):
                p = pos_s[j * tt + t]
                acc = acc + gate_s[j * tt + t] * stage[pl.ds(pl.multiple_of(p * sub, sub), sub), :]
            tok[pl.ds(pl.multiple_of(t * sub, sub), sub), :] = acc
            return carry

        lax.fori_loop(0, tt, gather, 0)
        out = tile_copy(tok, o_hbm.at[pl.ds(pl.multiple_of(ti * tt * sub, tt * sub), tt * sub)], sems.at[0])
        out.start()
        out.wait()


def _routed_experts(xt, pos, gate, off, cnt, w_gate, w_up, w_down):
    t, d = xt.shape
    tt = MOE_TILE
    sub = d // LANES
    n_tiles = t // tt
    stage_rows = TOP_K * tt + N_EXPERTS * SEG_ALIGN + MOE_CHUNK
    any_spec = pl.BlockSpec(memory_space=pl.ANY)
    grid_spec = pltpu.PrefetchScalarGridSpec(
        num_scalar_prefetch=2,
        grid=(n_tiles, N_EXPERTS),
        in_specs=[any_spec, any_spec, any_spec,
                  pl.BlockSpec((1, d, D_EXPERT), lambda ti, e, off, cnt: (e, 0, 0)),
                  pl.BlockSpec((1, d, D_EXPERT), lambda ti, e, off, cnt: (e, 0, 0)),
                  pl.BlockSpec((1, D_EXPERT, d), lambda ti, e, off, cnt: (e, 0, 0))],
        out_specs=any_spec,
        scratch_shapes=[pltpu.VMEM((stage_rows * sub, LANES), F32),
                        pltpu.VMEM((tt * sub, LANES), F32),
                        pltpu.SMEM((TOP_K * tt,), jnp.int32),
                        pltpu.SMEM((TOP_K * tt,), F32),
                        pltpu.SemaphoreType.DMA((3,))],
    )
    out = pl.pallas_call(
        functools.partial(_moe_tile_kernel, tt=tt),
        grid_spec=grid_spec,
        out_shape=jax.ShapeDtypeStruct((t * sub, LANES), F32),
        compiler_params=_params("arbitrary", "arbitrary"),
    )(off, cnt, xt.reshape(t * sub, LANES), pos, gate, w_gate, w_up, w_down)
    return out.reshape(t, d)


def _shared_ln_kernel(x_ref, r_ref, sg_ref, su_ref, sd_ref, g_ref, b_ref, o_ref):
    x = x_ref[...]
    xb = x.astype(BF16)
    h = jax.nn.silu(_dot(xb, sg_ref[...])) * _dot(xb, su_ref[...])
    f = r_ref[...] + _dot(h.astype(BF16), sd_ref[...])
    o_ref[...] = _layer_norm_rows(ALPHA * x + f, g_ref[...], b_ref[...])


def _shared_ln(xt, routed, sh_gate, sh_up, sh_down, g, b):
    t, d = xt.shape
    row = lambda i: (i, 0)
    fix = lambda i: (0, 0)
    return pl.pallas_call(
        _shared_ln_kernel,
        grid=(t // ROW_BLOCK,),
        in_specs=[pl.BlockSpec((ROW_BLOCK, d), row), pl.BlockSpec((ROW_BLOCK, d), row),
                  pl.BlockSpec((d, D_EXPERT), fix), pl.BlockSpec((d, D_EXPERT), fix), pl.BlockSpec((D_EXPERT, d), fix),
                  pl.BlockSpec((1, d), fix), pl.BlockSpec((1, d), fix)],
        out_specs=pl.BlockSpec((ROW_BLOCK, d), row),
        out_shape=jax.ShapeDtypeStruct((t, d), F32),
        compiler_params=_params("parallel"),
    )(xt, routed, sh_gate.astype(BF16), sh_up.astype(BF16), sh_down.astype(BF16), g.reshape(1, d), b.reshape(1, d))


def _moe_layer(x, w_router, router_bias, w_gate, w_up, w_down, sh_gate, sh_up, sh_down, ln_g, ln_b):
    b, s, d = x.shape
    t = b * s
    xt = x.reshape(t, d)
    assert t % MOE_TILE == 0
    pos, gate, off, cnt = _route_tiles(xt, w_router, router_bias)
    routed = _routed_experts(xt, pos, gate, off, cnt, w_gate.astype(BF16), w_up.astype(BF16), w_down.astype(BF16))
    return _shared_ln(xt, routed, sh_gate, sh_up, sh_down, ln_g, ln_b).reshape(b, s, d)


def kernel(x, moba_w_qkv, moba_w_o, rel_bias, rwkv_mix, rwkv_w_rkv, rwkv_w0, rwkv_w1, rwkv_w2,
           rwkv_a0, rwkv_a1, rwkv_a2, rwkv_v0, rwkv_v1, rwkv_v2, rwkv_g1, rwkv_g2, rwkv_k_k,
           rwkv_k_a, rwkv_r_k, rwkv_gn_g, rwkv_gn_b, rwkv_w_o, moe_w_router, moe_router_bias,
           moe_w_gate, moe_w_up, moe_w_down, moe_sh_gate, moe_sh_up, moe_sh_down,
           ln_mix_g, ln_mix_b, ln_ffn_g, ln_ffn_b):
    assert x.shape[-1] == D_MODEL
    own_bias, prev_bias = _moba_bias_tables(rel_bias)
    v_first = None
    for i in range(DEPTH):
        j = i // 2
        if i % 2 == 0:
            x = _moba_layer(x, moba_w_qkv[j], moba_w_o[j], own_bias, prev_bias, ln_mix_g[i], ln_mix_b[i])
        else:
            vres = None if j == 0 else (rwkv_v0[j - 1], rwkv_v1[j - 1], rwkv_v2[j - 1])
            x, v_first = _rwkv_layer(x, v_first, rwkv_mix[j], rwkv_w_rkv[j], rwkv_w0[j], rwkv_w1[j],
                                     rwkv_w2[j], rwkv_a0[j], rwkv_a1[j], rwkv_a2[j], rwkv_g1[j],
                                     rwkv_g2[j], rwkv_k_k[j], rwkv_k_a[j], rwkv_r_k[j],
                                     rwkv_gn_g[j], rwkv_gn_b[j], rwkv_w_o[j], vres, ln_mix_g[i], ln_mix_b[i])
        x = _moe_layer(x, moe_w_router[i], moe_router_bias[i], moe_w_gate[i], moe_w_up[i], moe_w_down[i],
                       moe_sh_gate[i], moe_sh_up[i], moe_sh_down[i], ln_ffn_g[i], ln_ffn_b[i])
    return x
```

```python
import functools
import math

import jax
import jax.numpy as jnp
import numpy as np
from jax import lax
from jax.experimental import pallas as pl
from jax.experimental.pallas import tpu as pltpu

F32 = jnp.float32
BF16 = jnp.bfloat16

D_MODEL = 1024
DEPTH = 4
ALPHA = (2 * DEPTH) ** 0.25
LN_EPS = 1e-5

HEAD_DIM = 64
N_HEADS = D_MODEL // HEAD_DIM
LANES = 128
N_PAIRS = D_MODEL // LANES
MOBA_BLOCK = 256
MOBA_TOPK = 3
N_BUCKETS = 32
MAX_DISTANCE = 128
MASKED = -1e30

GN_EPS = 64e-5
WKV_CHUNK = 64

N_EXPERTS = 64
N_GROUPS = 8
TOPK_GROUPS = 4
TOP_K = 8
D_EXPERT = 256
ROUTED_SCALE = 2.5
MOE_ROWS = 512

VMEM_LIMIT = 56 * 1024 * 1024
ROW_BLOCK = 512

_CONTRACT_LAST = (((1,), (1,)), ((), ()))
_CONTRACT_FIRST = (((0,), (0,)), ((), ()))


def _params(*sem):
    return pltpu.CompilerParams(dimension_semantics=sem, vmem_limit_bytes=VMEM_LIMIT)


def _dot(a, b):
    return jnp.dot(a, b, preferred_element_type=F32)


def _dot_t(a, b):
    return lax.dot_general(a, b, _CONTRACT_LAST, preferred_element_type=F32)


def _split3(x):
    hi = x.astype(BF16)
    r1 = x - hi.astype(F32)
    mid = r1.astype(BF16)
    lo = (r1 - mid.astype(F32)).astype(BF16)
    return hi, mid, lo


def _first_max(vals, idx, big):
    mx = jnp.max(vals, axis=0, keepdims=True)
    return mx, jnp.min(jnp.where(vals == mx, idx, big), axis=0, keepdims=True)


def _layer_norm_rows(y, g, b):
    mu = jnp.mean(y, axis=-1, keepdims=True)
    yc = y - mu
    var = jnp.mean(yc * yc, axis=-1, keepdims=True)
    return yc * lax.rsqrt(var + LN_EPS) * g + b


def _mm_kernel(x_ref, w_ref, o_ref):
    x = x_ref[...].astype(BF16)
    o_ref[...] = _dot(x, w_ref[...]).astype(o_ref.dtype)


def _matmul(x, w, out_dtype):
    m, k = x.shape
    n = w.shape[1]
    return pl.pallas_call(
        _mm_kernel,
        grid=(m // ROW_BLOCK,),
        in_specs=[pl.BlockSpec((ROW_BLOCK, k), lambda i: (i, 0)),
                  pl.BlockSpec((k, n), lambda i: (0, 0))],
        out_specs=pl.BlockSpec((ROW_BLOCK, n), lambda i: (i, 0)),
        out_shape=jax.ShapeDtypeStruct((m, n), out_dtype),
        compiler_params=_params("parallel"),
    )(x, w)


def _proj_ln_kernel(x_ref, a_ref, w_ref, g_ref, b_ref, o_ref):
    h = _dot(a_ref[...].astype(BF16), w_ref[...])
    o_ref[...] = _layer_norm_rows(ALPHA * x_ref[...] + h, g_ref[...], b_ref[...])


def _proj_gate_ln_kernel(x_ref, a_ref, m_ref, w_ref, g_ref, b_ref, o_ref):
    h = _dot((a_ref[...] * m_ref[...]).astype(BF16), w_ref[...])
    o_ref[...] = _layer_norm_rows(ALPHA * x_ref[...] + h, g_ref[...], b_ref[...])


def _proj_ln(x, a, w, g, b, mult=None):
    t, d = x.shape
    k = a.shape[1]
    row = lambda i: (i, 0)
    fix = lambda i: (0, 0)
    acts = [a] if mult is None else [a, mult]
    kern = _proj_ln_kernel if mult is None else _proj_gate_ln_kernel
    return pl.pallas_call(
        kern,
        grid=(t // ROW_BLOCK,),
        in_specs=[pl.BlockSpec((ROW_BLOCK, d), row)]
        + [pl.BlockSpec((ROW_BLOCK, k), row) for _ in acts]
        + [pl.BlockSpec((k, d), fix), pl.BlockSpec((1, d), fix), pl.BlockSpec((1, d), fix)],
        out_specs=pl.BlockSpec((ROW_BLOCK, d), row),
        out_shape=jax.ShapeDtypeStruct((t, d), F32),
        compiler_params=_params("parallel"),
    )(x, *acts, w, g.reshape(1, d), b.reshape(1, d))


def _rel_bucket_np(dist):
    n = np.maximum(dist, 0)
    max_exact = N_BUCKETS // 2
    nf = np.maximum(n, 1).astype(np.float32)
    large = max_exact + (np.log(nf / np.float32(max_exact)) / np.float32(math.log(MAX_DISTANCE / max_exact))
                         * np.float32(N_BUCKETS - max_exact)).astype(np.int32)
    return np.where(n < max_exact, n, np.minimum(large, N_BUCKETS - 1))


def _moba_bias_tables(rel_bias):
    off = np.arange(MOBA_BLOCK)
    rel = off[:, None] - off[None, :]
    assert int(_rel_bucket_np(np.array([MOBA_BLOCK + 1]))[0]) == N_BUCKETS - 1
    bias_hd = rel_bias.astype(F32).T
    far = bias_hd[:, N_BUCKETS - 1][:, None, None]
    own = bias_hd[:, _rel_bucket_np(rel)] - far
    own = jnp.where(jnp.asarray(rel >= 0)[None], own, MASKED)
    prev = bias_hd[:, _rel_bucket_np(rel + MOBA_BLOCK)] - far
    return own, prev


def _paired_loop(n, body):
    def two(i, carry):
        body(2 * i)
        body(2 * i + 1)
        return carry

    lax.fori_loop(0, lax.shift_right_logical(n, 1), two, 0)

    @pl.when(jnp.bitwise_and(n, 1) == 1)
    def _last():
        body(n - 1)


def _moba_kernel(q_ref, k_ref, v_ref, own_ref, prev_ref, o_ref, km_ref, s_ref, mx_ref, acc_ref, *, nblk):
    qb = pl.program_id(2)
    blk = MOBA_BLOCK
    nbp = km_ref.shape[1]
    lane = lax.broadcasted_iota(jnp.int32, (blk, LANES), 1)

    @pl.when(qb == 0)
    def _block_means():
        rows = [jnp.sum(k_ref[0, j * blk:(j + 1) * blk, :].astype(F32), axis=0, keepdims=True) * (1.0 / blk)
                for j in range(nblk)]
        if nbp > nblk:
            rows.append(jnp.zeros((nbp - nblk, LANES), F32))
        km = jnp.concatenate(rows, axis=0)
        lane_k = lax.broadcasted_iota(jnp.int32, (nbp, LANES), 1)
        km_ref[0] = jnp.where(lane_k < HEAD_DIM, km, 0.0)
        km_ref[1] = jnp.where(lane_k >= HEAD_DIM, km, 0.0)

    q = q_ref[0]
    blk_id = lax.broadcasted_iota(jnp.int32, (nbp, blk), 0)
    tgt_lane = lax.broadcasted_iota(jnp.int32, (nbp, LANES), 1)
    src_blk = lax.broadcasted_iota(jnp.int32, (nbp, LANES), 0)
    heads = []
    for h in range(2):
        own = (lane < HEAD_DIM) if h == 0 else (lane >= HEAD_DIM)
        goff = HEAD_DIM if h == 0 else 0
        kmf = km_ref[h]
        kmh = kmf.astype(BF16)
        kml = (kmf - kmh.astype(F32)).astype(BF16)
        gate_t = _dot_t(kmh, q) + _dot_t(kml, q)
        valid = blk_id < qb
        gm = jnp.where(valid, gate_t, -jnp.inf)
        hidden = jnp.where(valid, 1.0, 0.0)
        for _ in range(MOBA_TOPK):
            mx, ix = _first_max(gm, blk_id, nbp)
            pick = (blk_id == ix) & (mx > -jnp.inf)
            hidden = jnp.where(pick, 0.0, hidden)
            gm = jnp.where(pick, -jnp.inf, gm)
        place = jnp.where(tgt_lane == src_blk + goff, MASKED, 0.0).astype(BF16)
        gsel = lax.dot_general(hidden.astype(BF16), place, _CONTRACT_FIRST, preferred_element_type=F32)
        q_aug = jnp.where(own, q, gsel.astype(BF16))
        q_own = jnp.where(own, q, jnp.zeros_like(q))
        heads.append((own, goff, q_aug, q_own))

    def block_rows(j):
        return pl.ds(pl.multiple_of(j * blk, blk), blk)

    kj = k_ref[0, block_rows(qb), :]
    for h, (own, goff, q_aug, q_own) in enumerate(heads):
        s = _dot_t(q_own, kj) + own_ref[h]
        s_ref[h, qb] = s
        mx_ref[h] = s

    def past_logits(j, with_prev_bias):
        kj = k_ref[0, block_rows(j), :]
        for h, (own, goff, q_aug, q_own) in enumerate(heads):
            ej = jnp.where(lane == goff + j, 1.0, 0.0).astype(BF16)
            s = _dot_t(q_aug, jnp.where(own, kj, ej))
            if with_prev_bias:
                s = s + prev_ref[h]
            s_ref[h, j] = s
            mx_ref[h] = jnp.maximum(mx_ref[h], s)

    @pl.when(qb >= 1)
    def _prev_block():
        past_logits(qb - 1, True)

    _paired_loop(jnp.maximum(qb - 1, 0), lambda j: past_logits(j, False))

    for h in range(2):
        mm = mx_ref[h]
        row_max = jnp.max(jnp.maximum(mm[:, :LANES], mm[:, LANES:]), axis=-1, keepdims=True)
        mx_ref[h] = jnp.broadcast_to(row_max, (blk, blk))
    acc_ref[...] = jnp.zeros_like(acc_ref)

    def accumulate(j):
        vj = v_ref[0, block_rows(j), :]
        for h, (own, goff, q_aug, q_own) in enumerate(heads):
            p = jnp.exp(s_ref[h, j] - mx_ref[h])
            acc_ref[h] = acc_ref[h] + _dot(p.astype(BF16), jnp.where(own, vj, jnp.ones_like(vj)))

    _paired_loop(qb + 1, accumulate)

    a0 = acc_ref[0]
    a1 = acc_ref[1]
    o0 = a0 / pltpu.roll(a0, HEAD_DIM, axis=1)
    o1 = a1 / pltpu.roll(a1, HEAD_DIM, axis=1)
    o_ref[0] = jnp.where(lane < HEAD_DIM, o0, o1).astype(o_ref.dtype)


def _moba_attention(qkv, own_bias, prev_bias):
    b, s, _ = qkv.shape
    assert s % MOBA_BLOCK == 0
    nblk = s // MOBA_BLOCK
    assert nblk <= HEAD_DIM
    return pl.pallas_call(
        functools.partial(_moba_kernel, nblk=nblk),
        grid=(b, N_PAIRS, nblk),
        in_specs=[
            pl.BlockSpec((1, MOBA_BLOCK, LANES), lambda bi, hp, qb: (bi, qb, hp)),
            pl.BlockSpec((1, s, LANES), lambda bi, hp, qb: (bi, 0, N_PAIRS + hp)),
            pl.BlockSpec((1, s, LANES), lambda bi, hp, qb: (bi, 0, 2 * N_PAIRS + hp)),
            pl.BlockSpec((2, MOBA_BLOCK, MOBA_BLOCK), lambda bi, hp, qb: (hp, 0, 0)),
            pl.BlockSpec((2, MOBA_BLOCK, MOBA_BLOCK), lambda bi, hp, qb: (hp, 0, 0)),
        ],
        out_specs=pl.BlockSpec((1, MOBA_BLOCK, LANES), lambda bi, hp, qb: (bi, qb, hp)),
        out_shape=jax.ShapeDtypeStruct((b, s, D_MODEL), BF16),
        scratch_shapes=[pltpu.VMEM((2, -(-nblk // 8) * 8, LANES), F32),
                        pltpu.VMEM((2, nblk, MOBA_BLOCK, MOBA_BLOCK), F32),
                        pltpu.VMEM((2, MOBA_BLOCK, MOBA_BLOCK), F32),
                        pltpu.VMEM((2, MOBA_BLOCK, LANES), F32)],
        compiler_params=_params("parallel", "parallel", "arbitrary"),
    )(qkv, qkv, qkv, own_bias, prev_bias)


def _moba_layer(x, w_qkv, w_o, own_bias, prev_bias, ln_g, ln_b):
    b, s, d = x.shape
    xt = x.reshape(b * s, d)
    scale = jnp.concatenate([jnp.full((d,), HEAD_DIM ** -0.5, F32), jnp.ones((2 * d,), F32)])
    qkv = _matmul(xt, (w_qkv * scale).astype(BF16), BF16)
    o = _moba_attention(qkv.reshape(b, s, 3 * d), own_bias, prev_bias)
    return _proj_ln(xt, o.reshape(b * s, d), w_o.astype(BF16), ln_g, ln_b).reshape(b, s, d)


def _softplus(u):
    return jnp.maximum(u, 0.0) + jnp.log1p(jnp.exp(-jnp.abs(u)))


def _rwkv_proj_kernel(*refs, seq_blocks, has_vres):
    if has_vres:
        (x_ref, xp_ref, vf_ref, mix_ref, wr_ref, wk_ref, wv_ref, w0_ref, w1_ref, w2_ref, a0_ref, a1_ref, a2_ref,
         g1_ref, g2_ref, v0_ref, v1_ref, v2_ref, r_ref, k_ref, v_ref, lw_ref, a_ref, g_ref) = refs
    else:
        (x_ref, xp_ref, mix_ref, wr_ref, wk_ref, wv_ref, w0_ref, w1_ref, w2_ref, a0_ref, a1_ref, a2_ref,
         g1_ref, g2_ref, r_ref, k_ref, v_ref, lw_ref, a_ref, g_ref) = refs
    x = x_ref[...]
    rows = lax.broadcasted_iota(jnp.int32, x.shape, 0)
    first_in_seq = pl.program_id(0) % seq_blocks == 0
    carry_row = jnp.where(first_in_seq, 0.0, xp_ref[7:8, :])
    x_prev = jnp.where(rows == 0, carry_row, pltpu.roll(x, 1, axis=0))
    xx = x_prev - x

    def mixed(i):
        return (x + xx * mix_ref[i:i + 1, :]).astype(BF16)

    def lora(inp, w_in, w_out, act=None):
        mid = _dot(inp, w_in[...])
        if act is not None:
            mid = act(mid)
        return _dot(mid.astype(BF16), w_out[...])

    r_ref[...] = _dot(mixed(0), wr_ref[...])
    k_ref[...] = _dot(mixed(1), wk_ref[...])
    xv = mixed(2)
    v = _dot(xv, wv_ref[...])
    if has_vres:
        v = v + (vf_ref[...] - v) * jax.nn.sigmoid(v0_ref[...] + lora(xv, v1_ref, v2_ref))
    v_ref[...] = v
    w_log = -_softplus(-(w0_ref[...] + lora(mixed(3), w1_ref, w2_ref, jnp.tanh))) - 0.5
    lw_ref[...] = -jnp.exp(w_log)
    a_ref[...] = jax.nn.sigmoid(a0_ref[...] + lora(mixed(4), a1_ref, a2_ref))
    g_ref[...] = lora(mixed(5), g1_ref, g2_ref, jax.nn.sigmoid)


def _rwkv_proj(xt, seq_len, v_first, mix, w_rkv, w0, w1, w2, a0, a1, a2, g1, g2, vres):
    t, d = xt.shape
    assert seq_len % ROW_BLOCK == 0
    row = lambda i: (i, 0)
    fix = lambda i: (0, 0)
    prev8 = lambda i: (jnp.maximum(i * (ROW_BLOCK // 8) - 1, 0), 0)
    has_vres = vres is not None
    vec = lambda z: z.reshape(1, d)
    bf = lambda z: z.astype(BF16)
    ins = [xt, xt] + ([v_first] if has_vres else []) + [
        mix, bf(w_rkv[0]), bf(w_rkv[1]), bf(w_rkv[2]), vec(w0), bf(w1), bf(w2), vec(a0), bf(a1), bf(a2), bf(g1), bf(g2)]
    if has_vres:
        ins += [vec(vres[0]), bf(vres[1]), bf(vres[2])]
    specs = [pl.BlockSpec((ROW_BLOCK, d), row), pl.BlockSpec((8, d), prev8)]
    if has_vres:
        specs.append(pl.BlockSpec((ROW_BLOCK, d), row))
    specs += [pl.BlockSpec(z.shape, fix) for z in ins[len(specs):]]
    return pl.pallas_call(
        functools.partial(_rwkv_proj_kernel, seq_blocks=seq_len // ROW_BLOCK, has_vres=has_vres),
        grid=(t // ROW_BLOCK,),
        in_specs=specs,
        out_specs=[pl.BlockSpec((ROW_BLOCK, d), row)] * 6,
        out_shape=[jax.ShapeDtypeStruct((t, d), F32)] * 6,
        compiler_params=_params("parallel"),
    )(*ins)


def _wkv_kernel(r_ref, k_ref, v_ref, lw_ref, a_ref, kk_ref, ka_ref, rk_ref, gg_ref, gb_ref, o_ref, s_ref, *, n_chunks):
    c = WKV_CHUNK

    @pl.when(pl.program_id(2) == 0)
    def _reset():
        s_ref[...] = jnp.zeros_like(s_ref)

    lane = lax.broadcasted_iota(jnp.int32, (c, LANES), 1)
    row = lax.broadcasted_iota(jnp.int32, (c, LANES), 0)
    col = jnp.bitwise_and(lane, HEAD_DIM - 1)
    h0 = lane < HEAD_DIM
    sq_r = lax.broadcasted_iota(jnp.int32, (c, c), 0)
    sq_c = lax.broadcasted_iota(jnp.int32, (c, c), 1)
    tri_incl = jnp.where(sq_c <= sq_r, 1.0, 0.0).astype(BF16)
    eye = jnp.where(sq_c == sq_r, 1.0, 0.0)
    st_r = lax.broadcasted_iota(jnp.int32, (LANES, LANES), 0)
    st_c = lax.broadcasted_iota(jnp.int32, (LANES, LANES), 1)
    same_head = (st_r < HEAD_DIM) == (st_c < HEAD_DIM)
    k_k, k_a, r_k, gn_g, gn_b = kk_ref[...], ka_ref[...], rk_ref[...], gg_ref[...], gb_ref[...]

    def head_sum(z):
        s0 = jnp.sum(jnp.where(h0, z, 0.0), axis=-1, keepdims=True)
        s1 = jnp.sum(jnp.where(h0, 0.0, z), axis=-1, keepdims=True)
        return jnp.where(h0, s0, s1)

    chunks = range(n_chunks)
    sl = [slice(ci * c, (ci + 1) * c) for ci in chunks]
    owns = (h0, jnp.logical_not(h0))
    keys = [(ci, h) for ci in chunks for h in range(2)]
    r_all, k_all, v_all, lw_all, a_all = r_ref[...], k_ref[...], v_ref[...], lw_ref[...], a_ref[...]
    lw_parts = _split3(lw_all)
    cum, kh, at, rt, bt_kt, bh_kh, e_tot, v, vv = [], [], [], [], [], [], [], [], []
    for ci in chunks:
        rs = sl[ci]
        cum.append(sum(_dot(tri_incl, part[rs]) for part in lw_parts))
    for ci in chunks:
        rs = sl[ci]
        k, a, lw = k_all[rs], a_all[rs], lw_all[rs]
        kk = k * k_k
        kk = kk / jnp.maximum(jnp.sqrt(head_sum(kk * kk)), 1e-12)
        kh.append(k * (1.0 + (a - 1.0) * k_a))
        bb = kk * a
        l_end = cum[ci][c - 1:c, :]
        e_end = jnp.exp(l_end - cum[ci])
        e_neg = jnp.exp(-cum[ci])
        e_tot.append(jnp.exp(l_end))
        at.append(-kk * jnp.exp(cum[ci] - lw))
        rt.append(r_all[rs] * jnp.exp(cum[ci]))
        bt_kt.append(jnp.concatenate([bb * e_neg, kh[ci] * e_neg], axis=0).astype(BF16))
        bh_kh.append(jnp.concatenate([bb * e_end, kh[ci] * e_end], axis=0).astype(BF16))
        v.append(v_all[rs])
        vv.append(jnp.concatenate([v[ci], v[ci]], axis=0).astype(BF16))
    at_own, a_top, a_bot, pw, inv = {}, {}, {}, {}, {}
    for key in keys:
        ci, h = key
        at_own[key] = jnp.where(owns[h], at[ci], 0.0)
        lhs = jnp.concatenate([at_own[key], jnp.where(owns[h], rt[ci], 0.0)], axis=0).astype(BF16)
        p = _dot_t(lhs, bt_kt[ci])
        a_top[key] = jnp.where(col < row, p[:c, :], 0.0)
        a_bot[key] = jnp.where(col <= row, p[c:, :], 0.0).astype(BF16)
        pw[key] = a_top[key][:, :c]
        inv[key] = eye + pw[key]
    span = 2
    while span < c:
        for key in keys:
            pw_b = pw[key].astype(BF16)
            pw[key] = _dot(pw_b, pw_b)
        for key in keys:
            inv[key] = inv[key] + _dot(pw[key].astype(BF16), inv[key].astype(BF16))
        span *= 2
    inv_b = {key: inv[key].astype(BF16) for key in keys}
    akv = {key: _dot(jnp.where(h0, 0.0, a_top[key]).astype(BF16), vv[key[0]]) for key in keys}
    at2h = {key: _dot(inv_b[key], at_own[key].astype(BF16)) for key in keys}
    uvh = {key: _dot(inv_b[key], akv[key].astype(BF16)) for key in keys}
    at2 = [(at2h[ci, 0] + at2h[ci, 1]).astype(BF16) for ci in chunks]
    uv = [jnp.where(h0, uvh[ci, 0], uvh[ci, 1]) for ci in chunks]
    w_step = [jnp.where(same_head, lax.dot_general(at2[ci], bh_kh[ci][:c], _CONTRACT_FIRST,
                                                   preferred_element_type=F32), 0.0).astype(BF16) for ci in chunks]
    n_step = [jnp.where(same_head, lax.dot_general(jnp.concatenate([uv[ci], v[ci]], axis=0).astype(BF16), bh_kh[ci],
                                                   _CONTRACT_FIRST, preferred_element_type=F32), 0.0) for ci in chunks]
    state = s_ref[...]
    entering = []
    for ci in chunks:
        state_b = state.astype(BF16)
        entering.append(state_b)
        state = state * e_tot[ci] + _dot(state_b, w_step[ci]) + n_step[ci]
    s_ref[...] = state
    u = [_dot_t(at2[ci], entering[ci]) + uv[ci] for ci in chunks]
    y_state = [_dot_t(rt[ci].astype(BF16), entering[ci]) for ci in chunks]
    for ci in chunks:
        u_v = jnp.concatenate([u[ci], v[ci]], axis=0).astype(BF16)
        y = y_state[ci] + jnp.where(h0, _dot(a_bot[ci, 0], u_v), _dot(a_bot[ci, 1], u_v))
        mu = head_sum(y) * (1.0 / HEAD_DIM)
        yc = y - mu
        var = head_sum(yc * yc) * (1.0 / HEAD_DIM)
        yn = yc * lax.rsqrt(var + GN_EPS) * gn_g + gn_b
        o_ref[sl[ci], :] = yn + head_sum(r_all[sl[ci]] * kh[ci] * r_k) * v[ci]


WKV_BLOCK = 512


def _wkv(r, k, v, lw, a, k_k, k_a, r_k, gn_g, gn_b, batch, seq_len):
    t, d = r.shape
    assert seq_len % WKV_BLOCK == 0 and WKV_BLOCK % WKV_CHUNK == 0
    nt = seq_len // WKV_BLOCK
    tok = lambda bi, hp, ti: (bi * nt + ti, hp)
    par = lambda bi, hp, ti: (0, hp)
    vec = lambda z: z.reshape(1, d)
    return pl.pallas_call(
        functools.partial(_wkv_kernel, n_chunks=WKV_BLOCK // WKV_CHUNK),
        grid=(batch, N_PAIRS, nt),
        in_specs=[pl.BlockSpec((WKV_BLOCK, LANES), tok)] * 5 + [pl.BlockSpec((1, LANES), par)] * 5,
        out_specs=pl.BlockSpec((WKV_BLOCK, LANES), tok),
        out_shape=jax.ShapeDtypeStruct((t, d), F32),
        scratch_shapes=[pltpu.VMEM((LANES, LANES), F32)],
        compiler_params=_params("parallel", "parallel", "arbitrary"),
    )(r, k, v, lw, a, vec(k_k), vec(k_a), vec(r_k), vec(gn_g), vec(gn_b))


def _rwkv_layer(x, v_first, mix, w_rkv, w0, w1, w2, a0, a1, a2, g1, g2, k_k, k_a, r_k, gn_g, gn_b, w_o, vres,
                ln_g, ln_b):
    b, s, d = x.shape
    xt = x.reshape(b * s, d)
    r, k, v, lw, a, g = _rwkv_proj(xt, s, v_first, mix, w_rkv, w0, w1, w2, a0, a1, a2, g1, g2, vres)
    y = _wkv(r, k, v, lw, a, k_k, k_a, r_k, gn_g, gn_b, b, s)
    out = _proj_ln(xt, y, w_o.astype(BF16), ln_g, ln_b, mult=g)
    return out.reshape(b, s, d), (v if vres is None else v_first)


def _route_kernel(x_ref, w_ref, b_ref, pos_ref, gate_ref, off_ref, cnt_ref, *, tt):
    xh, xm, _ = _split3(x_ref[...])
    w = w_ref[...]
    wh = w.astype(BF16)
    wm = (w - wh.astype(F32)).astype(BF16)
    scores = jax.nn.sigmoid(_dot_t(wh, xh) + _dot_t(wh, xm) + _dot_t(wm, xh))
    biased = scores + b_ref[...]
    per = N_EXPERTS // N_GROUPS
    neg = -jnp.inf
    row_g = lax.broadcasted_iota(jnp.int32, (per, tt), 0)
    group_scores = []
    for g in range(N_GROUPS):
        grp = biased[g * per:(g + 1) * per, :]
        m1, i1 = _first_max(grp, row_g, per)
        m2 = jnp.max(jnp.where(row_g == i1, neg, grp), axis=0, keepdims=True)
        group_scores.append(m1 + m2)
    gsc = jnp.concatenate(group_scores, axis=0)
    row_n = lax.broadcasted_iota(jnp.int32, (N_GROUPS, tt), 0)
    keep = jnp.zeros((N_GROUPS, tt), F32)
    for _ in range(TOPK_GROUPS):
        _, ix = _first_max(gsc, row_n, N_GROUPS)
        pick = row_n == ix
        keep = jnp.where(pick, 1.0, keep)
        gsc = jnp.where(pick, neg, gsc)
    keep_e = jnp.concatenate([jnp.broadcast_to(keep[g:g + 1, :], (per, tt)) for g in range(N_GROUPS)], axis=0)
    cand = jnp.where(keep_e > 0.0, biased, neg)
    row_e = lax.broadcasted_iota(jnp.int32, (N_EXPERTS, tt), 0)
    picks = []
    for _ in range(TOP_K):
        _, ix = _first_max(cand, row_e, N_EXPERTS)
        pick = row_e == ix
        picks.append(pick)
        cand = jnp.where(pick, neg, cand)
    raw = [jnp.sum(jnp.where(p, scores, 0.0), axis=0, keepdims=True) for p in picks]
    total = raw[0]
    for r in raw[1:]:
        total = total + r
    gate_ref[0] = jnp.concatenate([r / total * ROUTED_SCALE for r in raw], axis=0)

    sel = jnp.zeros((N_EXPERTS, tt), F32)
    for p in picks:
        sel = jnp.where(p, 1.0, sel)
    cb = 256
    cr = lax.broadcasted_iota(jnp.int32, (cb, cb), 0)
    cc = lax.broadcasted_iota(jnp.int32, (cb, cb), 1)
    upper_incl = jnp.where(cr <= cc, 1.0, 0.0).astype(BF16)
    carry = jnp.zeros((N_EXPERTS, 1), F32)
    ranks = []
    for c0 in range(0, tt, cb):
        blk = sel[:, c0:c0 + cb]
        incl = _dot(blk.astype(BF16), upper_incl)
        ranks.append(incl - blk + carry)
        carry = carry + incl[:, cb - 1:cb]
    rank = jnp.concatenate(ranks, axis=1)
    count = jnp.broadcast_to(carry, (N_EXPERTS, LANES))
    aligned8 = jnp.floor((count + 7.0) * 0.125)
    er = lax.broadcasted_iota(jnp.int32, (N_EXPERTS, N_EXPERTS), 0)
    ec = lax.broadcasted_iota(jnp.int32, (N_EXPERTS, N_EXPERTS), 1)
    strict_lower = jnp.where(ec < er, 1.0, 0.0).astype(BF16)
    start = 8.0 * _dot(strict_lower, aligned8.astype(BF16))
    where_to = (start[:, :1] + rank) * float(D_MODEL // LANES)
    pos_ref[0] = jnp.concatenate(
        [jnp.sum(jnp.where(p, where_to, 0.0), axis=0, keepdims=True) for p in picks], axis=0).astype(jnp.int32)
    off_ref[0] = start.astype(jnp.int32)
    cnt_ref[0] = count.astype(jnp.int32)


MOE_TILE = 1024
MOE_CHUNK = 128
EXPERTS_PER_STEP = 2
SEG_ALIGN = 8


def _route_tiles(xt, w_router, router_bias):
    t, d = xt.shape
    tt = MOE_TILE
    n_tiles = t // tt
    tile3 = lambda i: (i, 0, 0)
    pos, gate, off, cnt = pl.pallas_call(
        functools.partial(_route_kernel, tt=tt),
        grid=(n_tiles,),
        in_specs=[pl.BlockSpec((tt, d), lambda i: (i, 0)),
                  pl.BlockSpec((N_EXPERTS, d), lambda i: (0, 0)),
                  pl.BlockSpec((N_EXPERTS, 1), lambda i: (0, 0))],
        out_specs=[pl.BlockSpec((1, TOP_K, tt), tile3), pl.BlockSpec((1, TOP_K, tt), tile3),
                   pl.BlockSpec((1, N_EXPERTS, LANES), tile3), pl.BlockSpec((1, N_EXPERTS, LANES), tile3)],
        out_shape=[jax.ShapeDtypeStruct((n_tiles, TOP_K, tt), jnp.int32),
                   jax.ShapeDtypeStruct((n_tiles, TOP_K, tt), F32),
                   jax.ShapeDtypeStruct((n_tiles, N_EXPERTS, LANES), jnp.int32),
                   jax.ShapeDtypeStruct((n_tiles, N_EXPERTS, LANES), jnp.int32)],
        compiler_params=_params("parallel"),
    )(xt, w_router.astype(F32).T, router_bias.astype(F32).reshape(N_EXPERTS, 1))
    token_major = lambda z: z.transpose(0, 2, 1).reshape(n_tiles, tt * TOP_K)
    return token_major(pos), token_major(gate), off[:, :, 0].reshape(-1), cnt[:, :, 0].reshape(-1)


def _moe_tile_kernel(off_ref, cnt_ref, x_hbm, pos_hbm, gate_hbm, wg_ref, wu_ref, wd_ref, o_hbm,
                     stage, tok, pos_s, gate_s, sems, *, tt):
    ti = pl.program_id(0)
    step = pl.program_id(1)
    sub = D_MODEL // LANES
    rows = MOE_CHUNK

    def tile_copy(src, dst, sem):
        return pltpu.make_async_copy(src, dst, sem)

    @pl.when(step == 0)
    def _distribute():
        copies = [tile_copy(x_hbm.at[pl.ds(pl.multiple_of(ti * tt * sub, tt * sub), tt * sub)], tok, sems.at[0]),
                  tile_copy(pos_hbm.at[ti], pos_s, sems.at[1]),
                  tile_copy(gate_hbm.at[ti], gate_s, sems.at[2])]
        for c in copies:
            c.start()
        for c in copies:
            c.wait()
        zero_group = jnp.zeros((SEG_ALIGN * sub, LANES), F32)

        def clear_tail(ex, carry):
            end = off_ref[ti * N_EXPERTS + ex] + cnt_ref[ti * N_EXPERTS + ex]
            g0 = pl.multiple_of(lax.shift_left(lax.shift_right_logical(end, 3), 3) * sub, SEG_ALIGN * sub)
            stage[pl.ds(g0, SEG_ALIGN * sub), :] = zero_group
            return carry

        lax.fori_loop(0, N_EXPERTS, clear_tail, 0)
        last = ti * N_EXPERTS + N_EXPERTS - 1
        total = off_ref[last] + lax.shift_left(lax.shift_right_logical(cnt_ref[last] + 7, 3), 3)
        t0 = pl.multiple_of(total * sub, SEG_ALIGN * sub)
        stage[pl.ds(t0, rows * sub), :] = jnp.zeros((rows * sub, LANES), F32)

        def place(t, carry):
            row = tok[pl.ds(pl.multiple_of(t * sub, sub), sub), :]
            for j in range(TOP_K):
                stage[pl.ds(pl.multiple_of(pos_s[t * TOP_K + j], sub), sub), :] = row
            return carry

        lax.fori_loop(0, tt, place, 0)

    row_id = lax.broadcasted_iota(jnp.int32, (rows, LANES), 0)
    shift = rows.bit_length() - 1
    experts = []
    for i in range(EXPERTS_PER_STEP):
        e = (ti * (N_EXPERTS // EXPERTS_PER_STEP) + step) * EXPERTS_PER_STEP + i
        off, cnt = off_ref[e], cnt_ref[e]
        experts.append((i, off, cnt, lax.shift_right_logical(cnt + rows - 1, shift)))
    n_steps = experts[0][3]
    for ex in experts[1:]:
        n_steps = jnp.maximum(n_steps, ex[3])

    def ffn_step(c, carry):
        loaded = []
        for i, off, cnt, n_mine in experts:
            c_mine = jnp.minimum(c, jnp.maximum(n_mine - 1, 0))
            base = pl.multiple_of((off + c_mine * rows) * sub, SEG_ALIGN * sub)
            loaded.append((base, [stage[pl.ds(base + k, rows, stride=sub), :] for k in range(sub)]))
        outs = []
        for (i, off, cnt, n_mine), (base, parts) in zip(experts, loaded):
            xb = jnp.concatenate(parts, axis=1).astype(BF16)
            h = jax.nn.silu(_dot(xb, wg_ref[i])) * _dot(xb, wu_ref[i])
            outs.append(_dot(h.astype(BF16), wd_ref[i]))
        for (i, off, cnt, n_mine), (base, parts), y in zip(experts, loaded, outs):
            live = row_id < cnt - c * rows
            for k in range(sub):
                stage[pl.ds(base + k, rows, stride=sub), :] = jnp.where(live, y[:, k * LANES:(k + 1) * LANES], parts[k])
        return carry

    lax.fori_loop(0, n_steps, ffn_step, 0)

    @pl.when(step == N_EXPERTS // EXPERTS_PER_STEP - 1)
    def _combine():
        def gather(t, carry):
            acc = jnp.zeros((sub, LANES), F32)
            for j in range(TOP_K):
                p = pl.multiple_of(pos_s[t * TOP_K + j], sub)
                acc = acc + gate_s[t * TOP_K + j] * stage[pl.ds(p, sub), :]
            tok[pl.ds(pl.multiple_of(t * sub, sub), sub), :] = acc
            return carry

        lax.fori_loop(0, tt, gather, 0)
        out = tile_copy(tok, o_hbm.at[pl.ds(pl.multiple_of(ti * tt * sub, tt * sub), tt * sub)], sems.at[0])
        out.start()
        out.wait()


def _routed_experts(xt, pos, gate, off, cnt, w_gate, w_up, w_down):
    t, d = xt.shape
    tt = MOE_TILE
    sub = d // LANES
    n_tiles = t // tt
    stage_rows = TOP_K * tt + N_EXPERTS * SEG_ALIGN + MOE_CHUNK
    any_spec = pl.BlockSpec(memory_space=pl.ANY)
    grid_spec = pltpu.PrefetchScalarGridSpec(
        num_scalar_prefetch=2,
        grid=(n_tiles, N_EXPERTS // EXPERTS_PER_STEP),
        in_specs=[any_spec, any_spec, any_spec,
                  pl.BlockSpec((EXPERTS_PER_STEP, d, D_EXPERT), lambda ti, e, off, cnt: (e, 0, 0)),
                  pl.BlockSpec((EXPERTS_PER_STEP, d, D_EXPERT), lambda ti, e, off, cnt: (e, 0, 0)),
                  pl.BlockSpec((EXPERTS_PER_STEP, D_EXPERT, d), lambda ti, e, off, cnt: (e, 0, 0))],
        out_specs=any_spec,
        scratch_shapes=[pltpu.VMEM((stage_rows * sub, LANES), F32),
                        pltpu.VMEM((tt * sub, LANES), F32),
                        pltpu.SMEM((TOP_K * tt,), jnp.int32),
                        pltpu.SMEM((TOP_K * tt,), F32),
                        pltpu.SemaphoreType.DMA((3,))],
    )
    out = pl.pallas_call(
        functools.partial(_moe_tile_kernel, tt=tt),
        grid_spec=grid_spec,
        out_shape=jax.ShapeDtypeStruct((t * sub, LANES), F32),
        compiler_params=_params("arbitrary", "arbitrary"),
    )(off, cnt, xt.reshape(t * sub, LANES), pos, gate, w_gate, w_up, w_down)
    return out.reshape(t, d)


def _shared_ln_kernel(x_ref, r_ref, sg_ref, su_ref, sd_ref, g_ref, b_ref, o_ref):
    x = x_ref[...]
    xb = x.astype(BF16)
    h = jax.nn.silu(_dot(xb, sg_ref[...])) * _dot(xb, su_ref[...])
    f = r_ref[...] + _dot(h.astype(BF16), sd_ref[...])
    o_ref[...] = _layer_norm_rows(ALPHA * x + f, g_ref[...], b_ref[...])


def _shared_ln(xt, routed, sh_gate, sh_up, sh_down, g, b):
    t, d = xt.shape
    row = lambda i: (i, 0)
    fix = lambda i: (0, 0)
    return pl.pallas_call(
        _shared_ln_kernel,
        grid=(t // ROW_BLOCK,),
        in_specs=[pl.BlockSpec((ROW_BLOCK, d), row), pl.BlockSpec((ROW_BLOCK, d), row),
                  pl.BlockSpec((d, D_EXPERT), fix), pl.BlockSpec((d, D_EXPERT), fix), pl.BlockSpec((D_EXPERT, d), fix),
                  pl.BlockSpec((1, d), fix), pl.BlockSpec((1, d), fix)],
        out_specs=pl.BlockSpec((ROW_BLOCK, d), row),
        out_shape=jax.ShapeDtypeStruct((t, d), F32),
        compiler_params=_params("parallel"),
    )(xt, routed, sh_gate.astype(BF16), sh_up.astype(BF16), sh_down.astype(BF16), g.reshape(1, d), b.reshape(1, d))


def _moe_layer(x, w_router, router_bias, w_gate, w_up, w_down, sh_gate, sh_up, sh_down, ln_g, ln_b):
    b, s, d = x.shape
    t = b * s
    xt = x.reshape(t, d)
    assert t % MOE_TILE == 0
    pos, gate, off, cnt = _route_tiles(xt, w_router, router_bias)
    routed = _routed_experts(xt, pos, gate, off, cnt, w_gate.astype(BF16), w_up.astype(BF16), w_down.astype(BF16))
    return _shared_ln(xt, routed, sh_gate, sh_up, sh_down, ln_g, ln_b).reshape(b, s, d)


def kernel(x, moba_w_qkv, moba_w_o, rel_bias, rwkv_mix, rwkv_w_rkv, rwkv_w0, rwkv_w1, rwkv_w2,
           rwkv_a0, rwkv_a1, rwkv_a2, rwkv_v0, rwkv_v1, rwkv_v2, rwkv_g1, rwkv_g2, rwkv_k_k,
           rwkv_k_a, rwkv_r_k, rwkv_gn_g, rwkv_gn_b, rwkv_w_o, moe_w_router, moe_router_bias,
           moe_w_gate, moe_w_up, moe_w_down, moe_sh_gate, moe_sh_up, moe_sh_down,
           ln_mix_g, ln_mix_b, ln_ffn_g, ln_ffn_b):
    assert x.shape[-1] == D_MODEL
    own_bias, prev_bias = _moba_bias_tables(rel_bias)
    v_first = None
    for i in range(DEPTH):
        j = i // 2
        if i % 2 == 0:
            x = _moba_layer(x, moba_w_qkv[j], moba_w_o[j], own_bias, prev_bias, ln_mix_g[i], ln_mix_b[i])
        else:
            vres = None if j == 0 else (rwkv_v0[j - 1], rwkv_v1[j - 1], rwkv_v2[j - 1])
            x, v_first = _rwkv_layer(x, v_first, rwkv_mix[j], rwkv_w_rkv[j], rwkv_w0[j], rwkv_w1[j],
                                     rwkv_w2[j], rwkv_a0[j], rwkv_a1[j], rwkv_a2[j], rwkv_g1[j],
                                     rwkv_g2[j], rwkv_k_k[j], rwkv_k_a[j], rwkv_r_k[j],
                                     rwkv_gn_g[j], rwkv_gn_b[j], rwkv_w_o[j], vres, ln_mix_g[i], ln_mix_b[i])
        x = _moe_layer(x, moe_w_router[i], moe_router_bias[i], moe_w_gate[i], moe_w_up[i], moe_w_down[i],
                       moe_sh_gate[i], moe_sh_up[i], moe_sh_down[i], ln_ffn_g[i], ln_ffn_b[i])
    return x
```

```python
import functools
import math

import jax
import jax.numpy as jnp
import numpy as np
from jax import lax
from jax.experimental import pallas as pl
from jax.experimental.pallas import tpu as pltpu

F32 = jnp.float32
BF16 = jnp.bfloat16

D_MODEL = 1024
DEPTH = 4
ALPHA = (2 * DEPTH) ** 0.25
LN_EPS = 1e-5

HEAD_DIM = 64
LANES = 128
N_PAIRS = D_MODEL // LANES
MOBA_BLOCK = 256
MOBA_TOPK = 3
N_BUCKETS = 32
MAX_DISTANCE = 128
MASKED = -1e30

GN_EPS = 64e-5
WKV_CHUNK = 64

N_EXPERTS = 64
N_GROUPS = 8
TOPK_GROUPS = 4
TOP_K = 8
D_EXPERT = 256
ROUTED_SCALE = 2.5

VMEM_LIMIT = 56 * 1024 * 1024
ROW_BLOCK = 512

_CONTRACT_LAST = (((1,), (1,)), ((), ()))
_CONTRACT_FIRST = (((0,), (0,)), ((), ()))


def _params(*sem):
    return pltpu.CompilerParams(dimension_semantics=sem, vmem_limit_bytes=VMEM_LIMIT)


def _dot(a, b):
    return jnp.dot(a, b, preferred_element_type=F32)


def _dot_t(a, b):
    return lax.dot_general(a, b, _CONTRACT_LAST, preferred_element_type=F32)


def _split3(x):
    hi = x.astype(BF16)
    r1 = x - hi.astype(F32)
    mid = r1.astype(BF16)
    lo = (r1 - mid.astype(F32)).astype(BF16)
    return hi, mid, lo


def _first_max(vals, idx, big):
    mx = jnp.max(vals, axis=0, keepdims=True)
    return mx, jnp.min(jnp.where(vals == mx, idx, big), axis=0, keepdims=True)


def _layer_norm_rows(y, g, b):
    mu = jnp.mean(y, axis=-1, keepdims=True)
    yc = y - mu
    var = jnp.mean(yc * yc, axis=-1, keepdims=True)
    return yc * lax.rsqrt(var + LN_EPS) * g + b


def _mm_kernel(x_ref, w_ref, o_ref):
    x = x_ref[...].astype(BF16)
    o_ref[...] = _dot(x, w_ref[...]).astype(o_ref.dtype)


def _matmul(x, w, out_dtype):
    m, k = x.shape
    n = w.shape[1]
    return pl.pallas_call(
        _mm_kernel,
        grid=(m // ROW_BLOCK,),
        in_specs=[pl.BlockSpec((ROW_BLOCK, k), lambda i: (i, 0)),
                  pl.BlockSpec((k, n), lambda i: (0, 0))],
        out_specs=pl.BlockSpec((ROW_BLOCK, n), lambda i: (i, 0)),
        out_shape=jax.ShapeDtypeStruct((m, n), out_dtype),
        compiler_params=_params("parallel"),
    )(x, w)


SUBROWS = D_MODEL // LANES


def _store_row_major(o8_ref, y):
    for k in range(SUBROWS):
        o8_ref[pl.ds(k, y.shape[0], stride=SUBROWS), :] = y[:, k * LANES:(k + 1) * LANES]


def _load_row_major(r8_ref, rows):
    return jnp.concatenate([r8_ref[pl.ds(k, rows, stride=SUBROWS), :] for k in range(SUBROWS)], axis=1)


def _proj_ln_kernel(x_ref, a_ref, w_ref, g_ref, b_ref, o_ref, o8_ref):
    h = _dot(a_ref[...].astype(BF16), w_ref[...])
    y = _layer_norm_rows(ALPHA * x_ref[...] + h, g_ref[...], b_ref[...])
    o_ref[...] = y
    _store_row_major(o8_ref, y)


def _proj_gate_ln_kernel(x_ref, a_ref, m_ref, w_ref, g_ref, b_ref, o_ref, o8_ref):
    h = _dot((a_ref[...] * m_ref[...]).astype(BF16), w_ref[...])
    y = _layer_norm_rows(ALPHA * x_ref[...] + h, g_ref[...], b_ref[...])
    o_ref[...] = y
    _store_row_major(o8_ref, y)


def _proj_ln(x, a, w, g, b, mult=None):
    t, d = x.shape
    k = a.shape[1]
    row = lambda i: (i, 0)
    fix = lambda i: (0, 0)
    acts = [a] if mult is None else [a, mult]
    kern = _proj_ln_kernel if mult is None else _proj_gate_ln_kernel
    return pl.pallas_call(
        kern,
        grid=(t // ROW_BLOCK,),
        in_specs=[pl.BlockSpec((ROW_BLOCK, d), row)]
        + [pl.BlockSpec((ROW_BLOCK, k), row) for _ in acts]
        + [pl.BlockSpec((k, d), fix), pl.BlockSpec((1, d), fix), pl.BlockSpec((1, d), fix)],
        out_specs=[pl.BlockSpec((ROW_BLOCK, d), row), pl.BlockSpec((ROW_BLOCK * SUBROWS, LANES), row)],
        out_shape=[jax.ShapeDtypeStruct((t, d), F32), jax.ShapeDtypeStruct((t * SUBROWS, LANES), F32)],
        compiler_params=_params("parallel"),
    )(x, *acts, w, g.reshape(1, d), b.reshape(1, d))


def _rel_bucket_np(dist):
    n = np.maximum(dist, 0)
    max_exact = N_BUCKETS // 2
    nf = np.maximum(n, 1).astype(np.float32)
    large = max_exact + (np.log(nf / np.float32(max_exact)) / np.float32(math.log(MAX_DISTANCE / max_exact))
                         * np.float32(N_BUCKETS - max_exact)).astype(np.int32)
    return np.where(n < max_exact, n, np.minimum(large, N_BUCKETS - 1))


def _moba_bias_tables(rel_bias):
    off = np.arange(MOBA_BLOCK)
    rel = off[:, None] - off[None, :]
    assert int(_rel_bucket_np(np.array([MOBA_BLOCK + 1]))[0]) == N_BUCKETS - 1
    shifted = (rel_bias - rel_bias[N_BUCKETS - 1:]).astype(F32)

    def lookup(buckets):
        onehot = (jnp.asarray(buckets.reshape(-1, 1)) == jnp.arange(N_BUCKETS)[None, :]).astype(F32)
        table = jnp.dot(onehot, shifted, precision=lax.Precision.HIGHEST)
        return table.T.reshape(-1, MOBA_BLOCK, MOBA_BLOCK)

    own = jnp.where(jnp.asarray(rel >= 0)[None], lookup(_rel_bucket_np(rel)), MASKED)
    return own, lookup(_rel_bucket_np(rel + MOBA_BLOCK))


def _paired_loop(n, body):
    def two(i, carry):
        body(2 * i)
        body(2 * i + 1)
        return carry

    lax.fori_loop(0, lax.shift_right_logical(n, 1), two, 0)

    @pl.when(jnp.bitwise_and(n, 1) == 1)
    def _last():
        body(n - 1)


def _moba_gate_kernel(q_ref, k_ref, g_ref, *, nblk):
    blk = MOBA_BLOCK
    nbp = -(-nblk // 8) * 8
    s = nblk * blk
    rows = [jnp.sum(k_ref[0, j * blk:(j + 1) * blk, :].astype(F32), axis=0, keepdims=True) * (1.0 / blk)
            for j in range(nblk)]
    if nbp > nblk:
        rows.append(jnp.zeros((nbp - nblk, LANES), F32))
    km = jnp.concatenate(rows, axis=0)
    lane_k = lax.broadcasted_iota(jnp.int32, (nbp, LANES), 1)
    q = q_ref[0]
    blk_id = lax.broadcasted_iota(jnp.int32, (nbp, s), 0)
    q_blk = lax.shift_right_logical(lax.broadcasted_iota(jnp.int32, (nbp, s), 1), blk.bit_length() - 1)
    valid = blk_id < q_blk
    src_blk = lax.broadcasted_iota(jnp.int32, (nbp, LANES), 0)
    total = None
    for h in range(2):
        kmf = jnp.where((lane_k < HEAD_DIM) if h == 0 else (lane_k >= HEAD_DIM), km, 0.0)
        kmh = kmf.astype(BF16)
        kml = (kmf - kmh.astype(F32)).astype(BF16)
        gm = jnp.where(valid, _dot_t(kmh, q) + _dot_t(kml, q), -jnp.inf)
        hidden = jnp.where(valid, 1.0, 0.0)
        for _ in range(MOBA_TOPK):
            mx, ix = _first_max(gm, blk_id, nbp)
            pick = (blk_id == ix) & (mx > -jnp.inf)
            hidden = jnp.where(pick, 0.0, hidden)
            gm = jnp.where(pick, -jnp.inf, gm)
        goff = HEAD_DIM if h == 0 else 0
        place = jnp.where(lane_k == src_blk + goff, MASKED, 0.0).astype(BF16)
        gsel = lax.dot_general(hidden.astype(BF16), place, _CONTRACT_FIRST, preferred_element_type=F32)
        total = gsel if total is None else total + gsel
    g_ref[0] = total.astype(g_ref.dtype)


def _moba_gates(qkv, nblk):
    b, s, _ = qkv.shape
    return pl.pallas_call(
        functools.partial(_moba_gate_kernel, nblk=nblk),
        grid=(b, N_PAIRS),
        in_specs=[pl.BlockSpec((1, s, LANES), lambda bi, hp: (bi, 0, hp)),
                  pl.BlockSpec((1, s, LANES), lambda bi, hp: (bi, 0, N_PAIRS + hp))],
        out_specs=pl.BlockSpec((1, s, LANES), lambda bi, hp: (bi, 0, hp)),
        out_shape=jax.ShapeDtypeStruct((b, s, D_MODEL), BF16),
        compiler_params=_params("parallel", "parallel"),
    )(qkv, qkv)


def _moba_kernel(q_ref, g_ref, k_ref, v_ref, own_ref, prev_ref, o_ref, s_ref, mx_ref, acc_ref):
    qb = pl.program_id(2)
    blk = MOBA_BLOCK
    lane = lax.broadcasted_iota(jnp.int32, (blk, LANES), 1)
    q = q_ref[0]
    g = g_ref[0]
    heads = []
    for h in range(2):
        own = (lane < HEAD_DIM) if h == 0 else (lane >= HEAD_DIM)
        goff = HEAD_DIM if h == 0 else 0
        q_aug = jnp.where(own, q, g)
        q_own = jnp.where(own, q, jnp.zeros_like(q))
        heads.append((own, goff, q_aug, q_own))

    def block_rows(j):
        return pl.ds(pl.multiple_of(j * blk, blk), blk)

    kj = k_ref[0, block_rows(qb), :]
    for h, (own, goff, q_aug, q_own) in enumerate(heads):
        s = _dot_t(q_own, kj) + own_ref[h]
        s_ref[h, qb] = s
        mx_ref[h] = s

    def past_logits(j, with_prev_bias):
        kj = k_ref[0, block_rows(j), :]
        for h, (own, goff, q_aug, q_own) in enumerate(heads):
            ej = jnp.where(lane == goff + j, 1.0, 0.0).astype(BF16)
            s = _dot_t(q_aug, jnp.where(own, kj, ej))
            if with_prev_bias:
                s = s + prev_ref[h]
            s_ref[h, j] = s
            mx_ref[h] = jnp.maximum(mx_ref[h], s)

    @pl.when(qb >= 1)
    def _prev_block():
        past_logits(qb - 1, True)

    _paired_loop(jnp.maximum(qb - 1, 0), lambda j: past_logits(j, False))

    for h in range(2):
        mm = mx_ref[h]
        row_max = jnp.max(jnp.maximum(mm[:, :LANES], mm[:, LANES:]), axis=-1, keepdims=True)
        mx_ref[h] = jnp.broadcast_to(row_max, (blk, blk))
    acc_ref[...] = jnp.zeros_like(acc_ref)

    def accumulate(j):
        vj = v_ref[0, block_rows(j), :]
        for h, (own, goff, q_aug, q_own) in enumerate(heads):
            p = jnp.exp(s_ref[h, j] - mx_ref[h])
            acc_ref[h] = acc_ref[h] + _dot(p.astype(BF16), jnp.where(own, vj, jnp.ones_like(vj)))

    _paired_loop(qb + 1, accumulate)

    a0 = acc_ref[0]
    a1 = acc_ref[1]
    o0 = a0 / pltpu.roll(a0, HEAD_DIM, axis=1)
    o1 = a1 / pltpu.roll(a1, HEAD_DIM, axis=1)
    o_ref[0] = jnp.where(lane < HEAD_DIM, o0, o1).astype(o_ref.dtype)


def _moba_attention(qkv, own_bias, prev_bias):
    b, s, _ = qkv.shape
    assert s % MOBA_BLOCK == 0
    nblk = s // MOBA_BLOCK
    assert nblk <= HEAD_DIM
    return pl.pallas_call(
        _moba_kernel,
        grid=(b, N_PAIRS, nblk),
        in_specs=[
            pl.BlockSpec((1, MOBA_BLOCK, LANES), lambda bi, hp, qb: (bi, qb, hp)),
            pl.BlockSpec((1, MOBA_BLOCK, LANES), lambda bi, hp, qb: (bi, qb, hp)),
            pl.BlockSpec((1, s, LANES), lambda bi, hp, qb: (bi, 0, N_PAIRS + hp)),
            pl.BlockSpec((1, s, LANES), lambda bi, hp, qb: (bi, 0, 2 * N_PAIRS + hp)),
            pl.BlockSpec((2, MOBA_BLOCK, MOBA_BLOCK), lambda bi, hp, qb: (hp, 0, 0)),
            pl.BlockSpec((2, MOBA_BLOCK, MOBA_BLOCK), lambda bi, hp, qb: (hp, 0, 0)),
        ],
        out_specs=pl.BlockSpec((1, MOBA_BLOCK, LANES), lambda bi, hp, qb: (bi, qb, hp)),
        out_shape=jax.ShapeDtypeStruct((b, s, D_MODEL), BF16),
        scratch_shapes=[pltpu.VMEM((2, nblk, MOBA_BLOCK, MOBA_BLOCK), F32),
                        pltpu.VMEM((2, MOBA_BLOCK, MOBA_BLOCK), F32),
                        pltpu.VMEM((2, MOBA_BLOCK, LANES), F32)],
        compiler_params=_params("parallel", "parallel", "arbitrary"),
    )(qkv, _moba_gates(qkv, nblk), qkv, qkv, own_bias, prev_bias)


def _moba_layer(xt, b, s, w_qkv, w_o, own_bias, prev_bias, ln_g, ln_b):
    d = xt.shape[1]
    scale = jnp.concatenate([jnp.full((d,), HEAD_DIM ** -0.5, F32), jnp.ones((2 * d,), F32)])
    qkv = _matmul(xt, (w_qkv * scale).astype(BF16), BF16)
    o = _moba_attention(qkv.reshape(b, s, 3 * d), own_bias, prev_bias)
    return _proj_ln(xt, o.reshape(b * s, d), w_o.astype(BF16), ln_g, ln_b)


def _softplus(u):
    return jnp.maximum(u, 0.0) + jnp.log1p(jnp.exp(-jnp.abs(u)))


def _rwkv_proj_kernel(*refs, seq_blocks, has_vres):
    if has_vres:
        (x_ref, xp_ref, vf_ref, mix_ref, wr_ref, wk_ref, wv_ref, w0_ref, w1_ref, w2_ref, a0_ref, a1_ref, a2_ref,
         g1_ref, g2_ref, v0_ref, v1_ref, v2_ref, r_ref, k_ref, v_ref, lw_ref, a_ref, g_ref) = refs
    else:
        (x_ref, xp_ref, mix_ref, wr_ref, wk_ref, wv_ref, w0_ref, w1_ref, w2_ref, a0_ref, a1_ref, a2_ref,
         g1_ref, g2_ref, r_ref, k_ref, v_ref, lw_ref, a_ref, g_ref) = refs
    x = x_ref[...]
    rows = lax.broadcasted_iota(jnp.int32, x.shape, 0)
    first_in_seq = pl.program_id(0) % seq_blocks == 0
    carry_row = jnp.where(first_in_seq, 0.0, xp_ref[7:8, :])
    x_prev = jnp.where(rows == 0, carry_row, pltpu.roll(x, 1, axis=0))
    xx = x_prev - x

    def mixed(i):
        return (x + xx * mix_ref[i:i + 1, :]).astype(BF16)

    def lora(inp, w_in, w_out, act=None):
        mid = _dot(inp, w_in[...])
        if act is not None:
            mid = act(mid)
        return _dot(mid.astype(BF16), w_out[...])

    r_ref[...] = _dot(mixed(0), wr_ref[...])
    k_ref[...] = _dot(mixed(1), wk_ref[...])
    xv = mixed(2)
    v = _dot(xv, wv_ref[...])
    if has_vres:
        v = v + (vf_ref[...] - v) * jax.nn.sigmoid(v0_ref[...] + lora(xv, v1_ref, v2_ref))
    v_ref[...] = v
    w_log = -_softplus(-(w0_ref[...] + lora(mixed(3), w1_ref, w2_ref, jnp.tanh))) - 0.5
    lw_ref[...] = -jnp.exp(w_log)
    a_ref[...] = jax.nn.sigmoid(a0_ref[...] + lora(mixed(4), a1_ref, a2_ref))
    g_ref[...] = lora(mixed(5), g1_ref, g2_ref, jax.nn.sigmoid)


def _rwkv_proj(xt, seq_len, v_first, mix, w_rkv, w0, w1, w2, a0, a1, a2, g1, g2, vres):
    t, d = xt.shape
    assert seq_len % ROW_BLOCK == 0
    row = lambda i: (i, 0)
    fix = lambda i: (0, 0)
    prev8 = lambda i: (jnp.maximum(i * (ROW_BLOCK // 8) - 1, 0), 0)
    has_vres = vres is not None
    vec = lambda z: z.reshape(1, d)
    bf = lambda z: z.astype(BF16)
    ins = [xt, xt] + ([v_first] if has_vres else []) + [
        mix, bf(w_rkv[0]), bf(w_rkv[1]), bf(w_rkv[2]), vec(w0), bf(w1), bf(w2), vec(a0), bf(a1), bf(a2), bf(g1), bf(g2)]
    if has_vres:
        ins += [vec(vres[0]), bf(vres[1]), bf(vres[2])]
    specs = [pl.BlockSpec((ROW_BLOCK, d), row), pl.BlockSpec((8, d), prev8)]
    if has_vres:
        specs.append(pl.BlockSpec((ROW_BLOCK, d), row))
    specs += [pl.BlockSpec(z.shape, fix) for z in ins[len(specs):]]
    return pl.pallas_call(
        functools.partial(_rwkv_proj_kernel, seq_blocks=seq_len // ROW_BLOCK, has_vres=has_vres),
        grid=(t // ROW_BLOCK,),
        in_specs=specs,
        out_specs=[pl.BlockSpec((ROW_BLOCK, d), row)] * 6,
        out_shape=[jax.ShapeDtypeStruct((t, d), F32)] * 6,
        compiler_params=_params("parallel"),
    )(*ins)


def _wkv_kernel(r_ref, k_ref, v_ref, lw_ref, a_ref, kk_ref, ka_ref, rk_ref, gg_ref, gb_ref, o_ref, s_ref, *, n_chunks):
    c = WKV_CHUNK

    @pl.when(pl.program_id(2) == 0)
    def _reset():
        s_ref[...] = jnp.zeros_like(s_ref)

    lane = lax.broadcasted_iota(jnp.int32, (c, LANES), 1)
    row = lax.broadcasted_iota(jnp.int32, (c, LANES), 0)
    col = jnp.bitwise_and(lane, HEAD_DIM - 1)
    h0 = lane < HEAD_DIM
    sq_r = lax.broadcasted_iota(jnp.int32, (c, c), 0)
    sq_c = lax.broadcasted_iota(jnp.int32, (c, c), 1)
    tri_incl = jnp.where(sq_c <= sq_r, 1.0, 0.0).astype(BF16)
    eye = jnp.where(sq_c == sq_r, 1.0, 0.0)
    st_r = lax.broadcasted_iota(jnp.int32, (LANES, LANES), 0)
    st_c = lax.broadcasted_iota(jnp.int32, (LANES, LANES), 1)
    same_head = (st_r < HEAD_DIM) == (st_c < HEAD_DIM)
    k_k, k_a, r_k, gn_g, gn_b = kk_ref[...], ka_ref[...], rk_ref[...], gg_ref[...], gb_ref[...]

    def head_sum(z):
        s0 = jnp.sum(jnp.where(h0, z, 0.0), axis=-1, keepdims=True)
        s1 = jnp.sum(jnp.where(h0, 0.0, z), axis=-1, keepdims=True)
        return jnp.where(h0, s0, s1)

    chunks = range(n_chunks)
    sl = [slice(ci * c, (ci + 1) * c) for ci in chunks]
    owns = (h0, jnp.logical_not(h0))
    keys = [(ci, h) for ci in chunks for h in range(2)]
    r_all, k_all, v_all, lw_all, a_all = r_ref[...], k_ref[...], v_ref[...], lw_ref[...], a_ref[...]
    lw_parts = _split3(lw_all)
    cum, kh, at, rt, bt_kt, bh_kh, e_tot, v, vv = [], [], [], [], [], [], [], [], []
    for ci in chunks:
        rs = sl[ci]
        cum.append(sum(_dot(tri_incl, part[rs]) for part in lw_parts))
    for ci in chunks:
        rs = sl[ci]
        k, a, lw = k_all[rs], a_all[rs], lw_all[rs]
        kk = k * k_k
        kk = kk / jnp.maximum(jnp.sqrt(head_sum(kk * kk)), 1e-12)
        kh.append(k * (1.0 + (a - 1.0) * k_a))
        bb = kk * a
        l_end = cum[ci][c - 1:c, :]
        e_end = jnp.exp(l_end - cum[ci])
        e_neg = jnp.exp(-cum[ci])
        e_tot.append(jnp.exp(l_end))
        at.append(-kk * jnp.exp(cum[ci] - lw))
        rt.append(r_all[rs] * jnp.exp(cum[ci]))
        bt_kt.append(jnp.concatenate([bb * e_neg, kh[ci] * e_neg], axis=0).astype(BF16))
        bh_kh.append(jnp.concatenate([bb * e_end, kh[ci] * e_end], axis=0).astype(BF16))
        v.append(v_all[rs])
        vv.append(jnp.concatenate([v[ci], v[ci]], axis=0).astype(BF16))
    at_own, a_top, a_bot, pw, inv = {}, {}, {}, {}, {}
    for key in keys:
        ci, h = key
        at_own[key] = jnp.where(owns[h], at[ci], 0.0)
        lhs = jnp.concatenate([at_own[key], jnp.where(owns[h], rt[ci], 0.0)], axis=0).astype(BF16)
        p = _dot_t(lhs, bt_kt[ci])
        a_top[key] = jnp.where(col < row, p[:c, :], 0.0)
        a_bot[key] = jnp.where(col <= row, p[c:, :], 0.0).astype(BF16)
        pw[key] = a_top[key][:, :c]
        inv[key] = eye + pw[key]
    span = 2
    while span < c:
        for key in keys:
            pw_b = pw[key].astype(BF16)
            pw[key] = _dot(pw_b, pw_b)
        for key in keys:
            inv[key] = inv[key] + _dot(pw[key].astype(BF16), inv[key].astype(BF16))
        span *= 2
    inv_b = {key: inv[key].astype(BF16) for key in keys}
    akv = {key: _dot(jnp.where(h0, 0.0, a_top[key]).astype(BF16), vv[key[0]]) for key in keys}
    at2h = {key: _dot(inv_b[key], at_own[key].astype(BF16)) for key in keys}
    uvh = {key: _dot(inv_b[key], akv[key].astype(BF16)) for key in keys}
    at2 = [(at2h[ci, 0] + at2h[ci, 1]).astype(BF16) for ci in chunks]
    uv = [jnp.where(h0, uvh[ci, 0], uvh[ci, 1]) for ci in chunks]
    w_step = [jnp.where(same_head, lax.dot_general(at2[ci], bh_kh[ci][:c], _CONTRACT_FIRST,
                                                   preferred_element_type=F32), 0.0).astype(BF16) for ci in chunks]
    n_step = [jnp.where(same_head, lax.dot_general(jnp.concatenate([uv[ci], v[ci]], axis=0).astype(BF16), bh_kh[ci],
                                                   _CONTRACT_FIRST, preferred_element_type=F32), 0.0) for ci in chunks]
    state = s_ref[...]
    entering = []
    for ci in chunks:
        state_b = state.astype(BF16)
        entering.append(state_b)
        state = state * e_tot[ci] + _dot(state_b, w_step[ci]) + n_step[ci]
    s_ref[...] = state
    u = [_dot_t(at2[ci], entering[ci]) + uv[ci] for ci in chunks]
    y_state = [_dot_t(rt[ci].astype(BF16), entering[ci]) for ci in chunks]
    for ci in chunks:
        u_v = jnp.concatenate([u[ci], v[ci]], axis=0).astype(BF16)
        y = y_state[ci] + jnp.where(h0, _dot(a_bot[ci, 0], u_v), _dot(a_bot[ci, 1], u_v))
        mu = head_sum(y) * (1.0 / HEAD_DIM)
        yc = y - mu
        var = head_sum(yc * yc) * (1.0 / HEAD_DIM)
        yn = yc * lax.rsqrt(var + GN_EPS) * gn_g + gn_b
        o_ref[sl[ci], :] = yn + head_sum(r_all[sl[ci]] * kh[ci] * r_k) * v[ci]


WKV_BLOCK = 512


def _wkv(r, k, v, lw, a, k_k, k_a, r_k, gn_g, gn_b, batch, seq_len):
    t, d = r.shape
    assert seq_len % WKV_BLOCK == 0 and WKV_BLOCK % WKV_CHUNK == 0
    nt = seq_len // WKV_BLOCK
    tok = lambda bi, hp, ti: (bi * nt + ti, hp)
    par = lambda bi, hp, ti: (0, hp)
    vec = lambda z: z.reshape(1, d)
    return pl.pallas_call(
        functools.partial(_wkv_kernel, n_chunks=WKV_BLOCK // WKV_CHUNK),
        grid=(batch, N_PAIRS, nt),
        in_specs=[pl.BlockSpec((WKV_BLOCK, LANES), tok)] * 5 + [pl.BlockSpec((1, LANES), par)] * 5,
        out_specs=pl.BlockSpec((WKV_BLOCK, LANES), tok),
        out_shape=jax.ShapeDtypeStruct((t, d), F32),
        scratch_shapes=[pltpu.VMEM((LANES, LANES), F32)],
        compiler_params=_params("parallel", "parallel", "arbitrary"),
    )(r, k, v, lw, a, vec(k_k), vec(k_a), vec(r_k), vec(gn_g), vec(gn_b))


def _rwkv_layer(xt, b, s, v_first, mix, w_rkv, w0, w1, w2, a0, a1, a2, g1, g2, k_k, k_a, r_k, gn_g, gn_b, w_o, vres,
                ln_g, ln_b):
    r, k, v, lw, a, g = _rwkv_proj(xt, s, v_first, mix, w_rkv, w0, w1, w2, a0, a1, a2, g1, g2, vres)
    y = _wkv(r, k, v, lw, a, k_k, k_a, r_k, gn_g, gn_b, b, s)
    return _proj_ln(xt, y, w_o.astype(BF16), ln_g, ln_b, mult=g), (v if vres is None else v_first)


def _route_kernel(x_ref, w_ref, b_ref, pos_ref, gate_ref, off_ref, cnt_ref, *, tt):
    xh, xm, _ = _split3(x_ref[...])
    w = w_ref[...]
    wh = w.astype(BF16)
    wm = (w - wh.astype(F32)).astype(BF16)
    scores = jax.nn.sigmoid(_dot_t(wh, xh) + _dot_t(wh, xm) + _dot_t(wm, xh))
    biased = scores + b_ref[...]
    per = N_EXPERTS // N_GROUPS
    neg = -jnp.inf
    row_g = lax.broadcasted_iota(jnp.int32, (per, tt), 0)
    group_scores = []
    for g in range(N_GROUPS):
        grp = biased[g * per:(g + 1) * per, :]
        m1, i1 = _first_max(grp, row_g, per)
        m2 = jnp.max(jnp.where(row_g == i1, neg, grp), axis=0, keepdims=True)
        group_scores.append(m1 + m2)
    gsc = jnp.concatenate(group_scores, axis=0)
    row_n = lax.broadcasted_iota(jnp.int32, (N_GROUPS, tt), 0)
    keep = jnp.zeros((N_GROUPS, tt), F32)
    for _ in range(TOPK_GROUPS):
        _, ix = _first_max(gsc, row_n, N_GROUPS)
        pick = row_n == ix
        keep = jnp.where(pick, 1.0, keep)
        gsc = jnp.where(pick, neg, gsc)
    keep_e = jnp.concatenate([jnp.broadcast_to(keep[g:g + 1, :], (per, tt)) for g in range(N_GROUPS)], axis=0)
    cand = jnp.where(keep_e > 0.0, biased, neg)
    row_e = lax.broadcasted_iota(jnp.int32, (N_EXPERTS, tt), 0)
    picks = []
    for _ in range(TOP_K):
        _, ix = _first_max(cand, row_e, N_EXPERTS)
        pick = row_e == ix
        picks.append(pick)
        cand = jnp.where(pick, neg, cand)
    raw = [jnp.sum(jnp.where(p, scores, 0.0), axis=0, keepdims=True) for p in picks]
    total = raw[0]
    for r in raw[1:]:
        total = total + r
    gate_ref[0] = jnp.concatenate([r / total * ROUTED_SCALE for r in raw], axis=0)

    sel = jnp.zeros((N_EXPERTS, tt), F32)
    for p in picks:
        sel = jnp.where(p, 1.0, sel)
    cb = 256
    cr = lax.broadcasted_iota(jnp.int32, (cb, cb), 0)
    cc = lax.broadcasted_iota(jnp.int32, (cb, cb), 1)
    upper_incl = jnp.where(cr <= cc, 1.0, 0.0).astype(BF16)
    carry = jnp.zeros((N_EXPERTS, 1), F32)
    ranks = []
    for c0 in range(0, tt, cb):
        blk = sel[:, c0:c0 + cb]
        incl = _dot(blk.astype(BF16), upper_incl)
        ranks.append(incl - blk + carry)
        carry = carry + incl[:, cb - 1:cb]
    rank = jnp.concatenate(ranks, axis=1)
    count = jnp.broadcast_to(carry, (N_EXPERTS, LANES))
    aligned8 = jnp.floor((count + 7.0) * 0.125)
    er = lax.broadcasted_iota(jnp.int32, (N_EXPERTS, N_EXPERTS), 0)
    ec = lax.broadcasted_iota(jnp.int32, (N_EXPERTS, N_EXPERTS), 1)
    strict_lower = jnp.where(ec < er, 1.0, 0.0).astype(BF16)
    start = 8.0 * _dot(strict_lower, aligned8.astype(BF16))
    where_to = (start[:, :1] + rank) * float(D_MODEL // LANES)
    pos_ref[0] = jnp.concatenate(
        [jnp.sum(jnp.where(p, where_to, 0.0), axis=0, keepdims=True) for p in picks], axis=0).astype(jnp.int32)
    off_ref[0] = start.astype(jnp.int32)
    cnt_ref[0] = count.astype(jnp.int32)


MOE_TILE = 1024
MOE_CHUNK = 160
EXPERTS_PER_STEP = 2
SEG_ALIGN = 8


def _route_tiles(xt, w_router, router_bias):
    t, d = xt.shape
    tt = MOE_TILE
    n_tiles = t // tt
    tile3 = lambda i: (i, 0, 0)
    pos, gate, off, cnt = pl.pallas_call(
        functools.partial(_route_kernel, tt=tt),
        grid=(n_tiles,),
        in_specs=[pl.BlockSpec((tt, d), lambda i: (i, 0)),
                  pl.BlockSpec((N_EXPERTS, d), lambda i: (0, 0)),
                  pl.BlockSpec((N_EXPERTS, 1), lambda i: (0, 0))],
        out_specs=[pl.BlockSpec((1, TOP_K, tt), tile3), pl.BlockSpec((1, TOP_K, tt), tile3),
                   pl.BlockSpec((1, N_EXPERTS, LANES), tile3), pl.BlockSpec((1, N_EXPERTS, LANES), tile3)],
        out_shape=[jax.ShapeDtypeStruct((n_tiles, TOP_K, tt), jnp.int32),
                   jax.ShapeDtypeStruct((n_tiles, TOP_K, tt), F32),
                   jax.ShapeDtypeStruct((n_tiles, N_EXPERTS, LANES), jnp.int32),
                   jax.ShapeDtypeStruct((n_tiles, N_EXPERTS, LANES), jnp.int32)],
        compiler_params=_params("parallel"),
    )(xt, w_router.astype(F32).T, router_bias.astype(F32).reshape(N_EXPERTS, 1))
    token_major = lambda z: z.transpose(0, 2, 1).reshape(n_tiles, tt * TOP_K)
    return token_major(pos), token_major(gate), off[:, :, 0].reshape(-1), cnt[:, :, 0].reshape(-1)


def _moe_tile_kernel(off_ref, cnt_ref, nch_ref, x_hbm, pos_hbm, gate_hbm, wg_ref, wu_ref, wd_ref, o_hbm,
                     stage, tok, pos_s, gate_s, sems, *, tt):
    ti = pl.program_id(0)
    step = pl.program_id(1)
    sub = D_MODEL // LANES
    rows = MOE_CHUNK

    def tile_copy(src, dst, sem):
        return pltpu.make_async_copy(src, dst, sem)

    @pl.when(step == 0)
    def _distribute():
        copies = [tile_copy(x_hbm.at[pl.ds(pl.multiple_of(ti * tt * sub, tt * sub), tt * sub)], tok, sems.at[0]),
                  tile_copy(pos_hbm.at[ti], pos_s, sems.at[1]),
                  tile_copy(gate_hbm.at[ti], gate_s, sems.at[2])]
        for c in copies:
            c.start()
        for c in copies:
            c.wait()
        zero_group = jnp.zeros((SEG_ALIGN * sub, LANES), F32)

        def clear_tail(ex, carry):
            end = off_ref[ti * N_EXPERTS + ex] + cnt_ref[ti * N_EXPERTS + ex]
            g0 = pl.multiple_of(lax.shift_left(lax.shift_right_logical(end, 3), 3) * sub, SEG_ALIGN * sub)
            stage[pl.ds(g0, SEG_ALIGN * sub), :] = zero_group
            return carry

        lax.fori_loop(0, N_EXPERTS, clear_tail, 0)
        last = ti * N_EXPERTS + N_EXPERTS - 1
        total = off_ref[last] + lax.shift_left(lax.shift_right_logical(cnt_ref[last] + 7, 3), 3)
        t0 = pl.multiple_of(total * sub, SEG_ALIGN * sub)
        stage[pl.ds(t0, rows * sub), :] = jnp.zeros((rows * sub, LANES), F32)

        def place(t2, carry):
            for t in (2 * t2, 2 * t2 + 1):
                row = tok[pl.ds(pl.multiple_of(t * sub, sub), sub), :]
                for j in range(TOP_K):
                    stage[pl.ds(pl.multiple_of(pos_s[t * TOP_K + j], sub), sub), :] = row
            return carry

        lax.fori_loop(0, tt // 2, place, 0)

    row_id = lax.broadcasted_iota(jnp.int32, (rows, LANES), 0)
    experts = []
    for i in range(EXPERTS_PER_STEP):
        e = (ti * (N_EXPERTS // EXPERTS_PER_STEP) + step) * EXPERTS_PER_STEP + i
        experts.append((i, off_ref[e], cnt_ref[e], nch_ref[e]))
    n_steps = experts[0][3]
    for ex in experts[1:]:
        n_steps = jnp.maximum(n_steps, ex[3])

    def ffn_step(c, carry):
        loaded = []
        for i, off, cnt, n_mine in experts:
            c_mine = jnp.minimum(c, jnp.maximum(n_mine - 1, 0))
            base = pl.multiple_of((off + c_mine * rows) * sub, SEG_ALIGN * sub)
            loaded.append((base, [stage[pl.ds(base + k, rows, stride=sub), :] for k in range(sub)]))
        outs = []
        for (i, off, cnt, n_mine), (base, parts) in zip(experts, loaded):
            xb = jnp.concatenate(parts, axis=1).astype(BF16)
            h = jax.nn.silu(_dot(xb, wg_ref[i])) * _dot(xb, wu_ref[i])
            outs.append(_dot(h.astype(BF16), wd_ref[i]))
        for (i, off, cnt, n_mine), (base, parts), y in zip(experts, loaded, outs):
            live = row_id < cnt - c * rows
            for k in range(sub):
                stage[pl.ds(base + k, rows, stride=sub), :] = jnp.where(live, y[:, k * LANES:(k + 1) * LANES], parts[k])
        return carry

    lax.fori_loop(0, n_steps, ffn_step, 0)

    @pl.when(step == N_EXPERTS // EXPERTS_PER_STEP - 1)
    def _combine():
        def gather(t2, carry):
            for t in (2 * t2, 2 * t2 + 1):
                acc = jnp.zeros((sub, LANES), F32)
                for j in range(TOP_K):
                    p = pl.multiple_of(pos_s[t * TOP_K + j], sub)
                    acc = acc + gate_s[t * TOP_K + j] * stage[pl.ds(p, sub), :]
                tok[pl.ds(pl.multiple_of(t * sub, sub), sub), :] = acc
            return carry

        lax.fori_loop(0, tt // 2, gather, 0)
        out = tile_copy(tok, o_hbm.at[pl.ds(pl.multiple_of(ti * tt * sub, tt * sub), tt * sub)], sems.at[0])
        out.start()
        out.wait()


def _routed_experts(x8, pos, gate, off, cnt, w_gate, w_up, w_down):
    sub = SUBROWS
    t, d = x8.shape[0] // sub, D_MODEL
    tt = MOE_TILE
    n_tiles = t // tt
    stage_rows = TOP_K * tt + N_EXPERTS * SEG_ALIGN + MOE_CHUNK
    any_spec = pl.BlockSpec(memory_space=pl.ANY)
    grid_spec = pltpu.PrefetchScalarGridSpec(
        num_scalar_prefetch=3,
        grid=(n_tiles, N_EXPERTS // EXPERTS_PER_STEP),
        in_specs=[any_spec, any_spec, any_spec,
                  pl.BlockSpec((EXPERTS_PER_STEP, d, D_EXPERT), lambda ti, e, *_: (e, 0, 0)),
                  pl.BlockSpec((EXPERTS_PER_STEP, d, D_EXPERT), lambda ti, e, *_: (e, 0, 0)),
                  pl.BlockSpec((EXPERTS_PER_STEP, D_EXPERT, d), lambda ti, e, *_: (e, 0, 0))],
        out_specs=any_spec,
        scratch_shapes=[pltpu.VMEM((stage_rows * sub, LANES), F32),
                        pltpu.VMEM((tt * sub, LANES), F32),
                        pltpu.SMEM((TOP_K * tt,), jnp.int32),
                        pltpu.SMEM((TOP_K * tt,), F32),
                        pltpu.SemaphoreType.DMA((3,))],
    )
    return pl.pallas_call(
        functools.partial(_moe_tile_kernel, tt=tt),
        grid_spec=grid_spec,
        out_shape=jax.ShapeDtypeStruct((t * sub, LANES), F32),
        compiler_params=_params("arbitrary", "arbitrary"),
    )(off, cnt, (cnt + MOE_CHUNK - 1) // MOE_CHUNK, x8, pos, gate, w_gate, w_up, w_down)


def _shared_ln_kernel(x_ref, r8_ref, sg_ref, su_ref, sd_ref, g_ref, b_ref, o_ref):
    x = x_ref[...]
    xb = x.astype(BF16)
    h = jax.nn.silu(_dot(xb, sg_ref[...])) * _dot(xb, su_ref[...])
    f = _load_row_major(r8_ref, x.shape[0]) + _dot(h.astype(BF16), sd_ref[...])
    o_ref[...] = _layer_norm_rows(ALPHA * x + f, g_ref[...], b_ref[...])


def _shared_ln(xt, routed8, sh_gate, sh_up, sh_down, g, b):
    t, d = xt.shape
    row = lambda i: (i, 0)
    fix = lambda i: (0, 0)
    return pl.pallas_call(
        _shared_ln_kernel,
        grid=(t // ROW_BLOCK,),
        in_specs=[pl.BlockSpec((ROW_BLOCK, d), row), pl.BlockSpec((ROW_BLOCK * SUBROWS, LANES), row),
                  pl.BlockSpec((d, D_EXPERT), fix), pl.BlockSpec((d, D_EXPERT), fix), pl.BlockSpec((D_EXPERT, d), fix),
                  pl.BlockSpec((1, d), fix), pl.BlockSpec((1, d), fix)],
        out_specs=pl.BlockSpec((ROW_BLOCK, d), row),
        out_shape=jax.ShapeDtypeStruct((t, d), F32),
        compiler_params=_params("parallel"),
    )(xt, routed8, sh_gate.astype(BF16), sh_up.astype(BF16), sh_down.astype(BF16), g.reshape(1, d), b.reshape(1, d))


def _moe_layer(xt, x8, w_router, router_bias, w_gate, w_up, w_down, sh_gate, sh_up, sh_down, ln_g, ln_b):
    assert xt.shape[0] % MOE_TILE == 0
    pos, gate, off, cnt = _route_tiles(xt, w_router, router_bias)
    routed8 = _routed_experts(x8, pos, gate, off, cnt, w_gate.astype(BF16), w_up.astype(BF16), w_down.astype(BF16))
    return _shared_ln(xt, routed8, sh_gate, sh_up, sh_down, ln_g, ln_b)


def kernel(x, moba_w_qkv, moba_w_o, rel_bias, rwkv_mix, rwkv_w_rkv, rwkv_w0, rwkv_w1, rwkv_w2,
           rwkv_a0, rwkv_a1, rwkv_a2, rwkv_v0, rwkv_v1, rwkv_v2, rwkv_g1, rwkv_g2, rwkv_k_k,
           rwkv_k_a, rwkv_r_k, rwkv_gn_g, rwkv_gn_b, rwkv_w_o, moe_w_router, moe_router_bias,
           moe_w_gate, moe_w_up, moe_w_down, moe_sh_gate, moe_sh_up, moe_sh_down,
           ln_mix_g, ln_mix_b, ln_ffn_g, ln_ffn_b):
    assert x.shape[-1] == D_MODEL
    own_bias, prev_bias = _moba_bias_tables(rel_bias)
    b, s, d = x.shape
    xt = x.reshape(b * s, d)
    v_first = None
    for i in range(DEPTH):
        j = i // 2
        if i % 2 == 0:
            xt, x8 = _moba_layer(xt, b, s, moba_w_qkv[j], moba_w_o[j], own_bias, prev_bias, ln_mix_g[i], ln_mix_b[i])
        else:
            vres = None if j == 0 else (rwkv_v0[j - 1], rwkv_v1[j - 1], rwkv_v2[j - 1])
            (xt, x8), v_first = _rwkv_layer(xt, b, s, v_first, rwkv_mix[j], rwkv_w_rkv[j], rwkv_w0[j], rwkv_w1[j],
                                            rwkv_w2[j], rwkv_a0[j], rwkv_a1[j], rwkv_a2[j], rwkv_g1[j],
                                            rwkv_g2[j], rwkv_k_k[j], rwkv_k_a[j], rwkv_r_k[j],
                                            rwkv_gn_g[j], rwkv_gn_b[j], rwkv_w_o[j], vres, ln_mix_g[i], ln_mix_b[i])
        xt = _moe_layer(xt, x8, moe_w_router[i], moe_router_bias[i], moe_w_gate[i], moe_w_up[i], moe_w_down[i],
                        moe_sh_gate[i], moe_sh_up[i], moe_sh_down[i], ln_ffn_g[i], ln_ffn_b[i])
    return xt.reshape(b, s, d)
```

```python
import functools
import math

import jax
import jax.numpy as jnp
import numpy as np
from jax import lax
from jax.experimental import pallas as pl
from jax.experimental.pallas import tpu as pltpu

F32 = jnp.float32
BF16 = jnp.bfloat16

D_MODEL = 1024
DEPTH = 4
ALPHA = (2 * DEPTH) ** 0.25
LN_EPS = 1e-5

HEAD_DIM = 64
LANES = 128
N_PAIRS = D_MODEL // LANES
MOBA_BLOCK = 256
MOBA_TOPK = 3
N_BUCKETS = 32
MAX_DISTANCE = 128
MASKED = -1e30

GN_EPS = 64e-5
WKV_CHUNK = 64

N_EXPERTS = 64
N_GROUPS = 8
TOPK_GROUPS = 4
TOP_K = 8
D_EXPERT = 256
ROUTED_SCALE = 2.5

VMEM_LIMIT = 56 * 1024 * 1024
ROW_BLOCK = 512

_CONTRACT_LAST = (((1,), (1,)), ((), ()))
_CONTRACT_FIRST = (((0,), (0,)), ((), ()))


def _params(*sem):
    return pltpu.CompilerParams(dimension_semantics=sem, vmem_limit_bytes=VMEM_LIMIT)


def _dot(a, b):
    return jnp.dot(a, b, preferred_element_type=F32)


def _dot_t(a, b):
    return lax.dot_general(a, b, _CONTRACT_LAST, preferred_element_type=F32)


def _split3(x):
    hi = x.astype(BF16)
    r1 = x - hi.astype(F32)
    mid = r1.astype(BF16)
    lo = (r1 - mid.astype(F32)).astype(BF16)
    return hi, mid, lo


def _first_max(vals, idx, big):
    mx = jnp.max(vals, axis=0, keepdims=True)
    return mx, jnp.min(jnp.where(vals == mx, idx, big), axis=0, keepdims=True)


def _layer_norm_rows(y, g, b):
    mu = jnp.mean(y, axis=-1, keepdims=True)
    yc = y - mu
    var = jnp.mean(yc * yc, axis=-1, keepdims=True)
    return yc * lax.rsqrt(var + LN_EPS) * g + b


def _mm_kernel(x_ref, w_ref, o_ref):
    x = x_ref[...].astype(BF16)
    o_ref[...] = _dot(x, w_ref[...]).astype(o_ref.dtype)


def _matmul(x, w, out_dtype):
    m, k = x.shape
    n = w.shape[1]
    return pl.pallas_call(
        _mm_kernel,
        grid=(m // ROW_BLOCK,),
        in_specs=[pl.BlockSpec((ROW_BLOCK, k), lambda i: (i, 0)),
                  pl.BlockSpec((k, n), lambda i: (0, 0))],
        out_specs=pl.BlockSpec((ROW_BLOCK, n), lambda i: (i, 0)),
        out_shape=jax.ShapeDtypeStruct((m, n), out_dtype),
        compiler_params=_params("parallel"),
    )(x, w)


SUBROWS = D_MODEL // LANES


def _store_row_major(o8_ref, y):
    for k in range(SUBROWS):
        o8_ref[pl.ds(k, y.shape[0], stride=SUBROWS), :] = y[:, k * LANES:(k + 1) * LANES]


def _load_row_major(r8_ref, rows):
    return jnp.concatenate([r8_ref[pl.ds(k, rows, stride=SUBROWS), :] for k in range(SUBROWS)], axis=1)


def _proj_ln_kernel(x_ref, a_ref, w_ref, g_ref, b_ref, o_ref, o8_ref):
    h = _dot(a_ref[...].astype(BF16), w_ref[...])
    y = _layer_norm_rows(ALPHA * x_ref[...] + h, g_ref[...], b_ref[...])
    o_ref[...] = y
    _store_row_major(o8_ref, y)


def _proj_gate_ln_kernel(x_ref, a_ref, m_ref, w_ref, g_ref, b_ref, o_ref, o8_ref):
    h = _dot((a_ref[...] * m_ref[...]).astype(BF16), w_ref[...])
    y = _layer_norm_rows(ALPHA * x_ref[...] + h, g_ref[...], b_ref[...])
    o_ref[...] = y
    _store_row_major(o8_ref, y)


def _proj_ln(x, a, w, g, b, mult=None):
    t, d = x.shape
    k = a.shape[1]
    row = lambda i: (i, 0)
    fix = lambda i: (0, 0)
    acts = [a] if mult is None else [a, mult]
    kern = _proj_ln_kernel if mult is None else _proj_gate_ln_kernel
    return pl.pallas_call(
        kern,
        grid=(t // ROW_BLOCK,),
        in_specs=[pl.BlockSpec((ROW_BLOCK, d), row)]
        + [pl.BlockSpec((ROW_BLOCK, k), row) for _ in acts]
        + [pl.BlockSpec((k, d), fix), pl.BlockSpec((1, d), fix), pl.BlockSpec((1, d), fix)],
        out_specs=[pl.BlockSpec((ROW_BLOCK, d), row), pl.BlockSpec((ROW_BLOCK * SUBROWS, LANES), row)],
        out_shape=[jax.ShapeDtypeStruct((t, d), F32), jax.ShapeDtypeStruct((t * SUBROWS, LANES), F32)],
        compiler_params=_params("parallel"),
    )(x, *acts, w, g.reshape(1, d), b.reshape(1, d))


def _rel_bucket_np(dist):
    n = np.maximum(dist, 0)
    max_exact = N_BUCKETS // 2
    nf = np.maximum(n, 1).astype(np.float32)
    large = max_exact + (np.log(nf / np.float32(max_exact)) / np.float32(math.log(MAX_DISTANCE / max_exact))
                         * np.float32(N_BUCKETS - max_exact)).astype(np.int32)
    return np.where(n < max_exact, n, np.minimum(large, N_BUCKETS - 1))


def _moba_bias_tables(rel_bias):
    off = np.arange(MOBA_BLOCK)
    rel = off[:, None] - off[None, :]
    assert int(_rel_bucket_np(np.array([MOBA_BLOCK + 1]))[0]) == N_BUCKETS - 1
    shifted = (rel_bias - rel_bias[N_BUCKETS - 1:]).astype(F32)

    def lookup(buckets):
        onehot = (jnp.asarray(buckets.reshape(-1, 1)) == jnp.arange(N_BUCKETS)[None, :]).astype(F32)
        table = jnp.dot(onehot, shifted, precision=lax.Precision.HIGHEST)
        return table.T.reshape(-1, MOBA_BLOCK, MOBA_BLOCK)

    own = jnp.where(jnp.asarray(rel >= 0)[None], lookup(_rel_bucket_np(rel)), MASKED)
    return own, lookup(_rel_bucket_np(rel + MOBA_BLOCK))


def _grouped_loop(n, body):
    def four(i, carry):
        for u in range(4):
            body(4 * i + u)
        return carry

    lax.fori_loop(0, lax.shift_right_logical(n, 2), four, 0)
    done = jnp.bitwise_and(n, -4)

    @pl.when(jnp.bitwise_and(n, 2) == 2)
    def _two_more():
        body(done)
        body(done + 1)

    @pl.when(jnp.bitwise_and(n, 1) == 1)
    def _last():
        body(n - 1)


def _moba_gate_kernel(q_ref, k_ref, g_ref, *, nblk):
    blk = MOBA_BLOCK
    nbp = -(-nblk // 8) * 8
    s = nblk * blk
    rows = [jnp.sum(k_ref[0, j * blk:(j + 1) * blk, :].astype(F32), axis=0, keepdims=True) * (1.0 / blk)
            for j in range(nblk)]
    if nbp > nblk:
        rows.append(jnp.zeros((nbp - nblk, LANES), F32))
    km = jnp.concatenate(rows, axis=0)
    lane_k = lax.broadcasted_iota(jnp.int32, (nbp, LANES), 1)
    q = q_ref[0]
    blk_id = lax.broadcasted_iota(jnp.int32, (nbp, s), 0)
    q_blk = lax.shift_right_logical(lax.broadcasted_iota(jnp.int32, (nbp, s), 1), blk.bit_length() - 1)
    valid = blk_id < q_blk
    src_blk = lax.broadcasted_iota(jnp.int32, (nbp, LANES), 0)
    total = None
    for h in range(2):
        kmf = jnp.where((lane_k < HEAD_DIM) if h == 0 else (lane_k >= HEAD_DIM), km, 0.0)
        kmh = kmf.astype(BF16)
        kml = (kmf - kmh.astype(F32)).astype(BF16)
        gm = jnp.where(valid, _dot_t(kmh, q) + _dot_t(kml, q), -jnp.inf)
        hidden = jnp.where(valid, 1.0, 0.0)
        for _ in range(MOBA_TOPK):
            mx, ix = _first_max(gm, blk_id, nbp)
            pick = (blk_id == ix) & (mx > -jnp.inf)
            hidden = jnp.where(pick, 0.0, hidden)
            gm = jnp.where(pick, -jnp.inf, gm)
        goff = HEAD_DIM if h == 0 else 0
        place = jnp.where(lane_k == src_blk + goff, MASKED, 0.0).astype(BF16)
        gsel = lax.dot_general(hidden.astype(BF16), place, _CONTRACT_FIRST, preferred_element_type=F32)
        total = gsel if total is None else total + gsel
    g_ref[0] = total.astype(g_ref.dtype)


def _moba_gates(qkv, nblk):
    b, s, _ = qkv.shape
    return pl.pallas_call(
        functools.partial(_moba_gate_kernel, nblk=nblk),
        grid=(b, N_PAIRS),
        in_specs=[pl.BlockSpec((1, s, LANES), lambda bi, hp: (bi, 0, hp)),
                  pl.BlockSpec((1, s, LANES), lambda bi, hp: (bi, 0, N_PAIRS + hp))],
        out_specs=pl.BlockSpec((1, s, LANES), lambda bi, hp: (bi, 0, hp)),
        out_shape=jax.ShapeDtypeStruct((b, s, D_MODEL), BF16),
        compiler_params=_params("parallel", "parallel"),
    )(qkv, qkv)


def _moba_kernel(q_ref, g_ref, k_ref, v_ref, own_ref, prev_ref, o_ref, s_ref, mx_ref, acc_ref):
    qb = pl.program_id(2)
    blk = MOBA_BLOCK
    lane = lax.broadcasted_iota(jnp.int32, (blk, LANES), 1)
    q = q_ref[0]
    g = g_ref[0]
    heads = []
    for h in range(2):
        own = (lane < HEAD_DIM) if h == 0 else (lane >= HEAD_DIM)
        goff = HEAD_DIM if h == 0 else 0
        q_aug = jnp.where(own, q, g)
        q_own = jnp.where(own, q, jnp.zeros_like(q))
        heads.append((own, goff, q_aug, q_own))

    def block_rows(j):
        return pl.ds(pl.multiple_of(j * blk, blk), blk)

    kj = k_ref[0, block_rows(qb), :]
    for h, (own, goff, q_aug, q_own) in enumerate(heads):
        s = _dot_t(q_own, kj) + own_ref[h]
        s_ref[h, qb] = s
        mx_ref[h] = s

    def past_logits(j, with_prev_bias):
        kj = k_ref[0, block_rows(j), :]
        for h, (own, goff, q_aug, q_own) in enumerate(heads):
            ej = jnp.where(lane == goff + j, 1.0, 0.0).astype(BF16)
            s = _dot_t(q_aug, jnp.where(own, kj, ej))
            if with_prev_bias:
                s = s + prev_ref[h]
            s_ref[h, j] = s
            mx_ref[h] = jnp.maximum(mx_ref[h], s)

    @pl.when(qb >= 1)
    def _prev_block():
        past_logits(qb - 1, True)

    _grouped_loop(jnp.maximum(qb - 1, 0), lambda j: past_logits(j, False))

    for h in range(2):
        mm = mx_ref[h]
        row_max = jnp.max(jnp.maximum(mm[:, :LANES], mm[:, LANES:]), axis=-1, keepdims=True)
        mx_ref[h] = jnp.broadcast_to(row_max, (blk, blk))
    acc_ref[...] = jnp.zeros_like(acc_ref)

    def accumulate(j):
        vj = v_ref[0, block_rows(j), :]
        for h, (own, goff, q_aug, q_own) in enumerate(heads):
            p = jnp.exp(s_ref[h, j] - mx_ref[h])
            acc_ref[h] = acc_ref[h] + _dot(p.astype(BF16), jnp.where(own, vj, jnp.ones_like(vj)))

    _grouped_loop(qb + 1, accumulate)

    a0 = acc_ref[0]
    a1 = acc_ref[1]
    o0 = a0 / pltpu.roll(a0, HEAD_DIM, axis=1)
    o1 = a1 / pltpu.roll(a1, HEAD_DIM, axis=1)
    o_ref[0] = jnp.where(lane < HEAD_DIM, o0, o1).astype(o_ref.dtype)


def _moba_attention(qkv, own_bias, prev_bias):
    b, s, _ = qkv.shape
    assert s % MOBA_BLOCK == 0
    nblk = s // MOBA_BLOCK
    assert nblk <= HEAD_DIM
    return pl.pallas_call(
        _moba_kernel,
        grid=(b, N_PAIRS, nblk),
        in_specs=[
            pl.BlockSpec((1, MOBA_BLOCK, LANES), lambda bi, hp, qb: (bi, qb, hp)),
            pl.BlockSpec((1, MOBA_BLOCK, LANES), lambda bi, hp, qb: (bi, qb, hp)),
            pl.BlockSpec((1, s, LANES), lambda bi, hp, qb: (bi, 0, N_PAIRS + hp)),
            pl.BlockSpec((1, s, LANES), lambda bi, hp, qb: (bi, 0, 2 * N_PAIRS + hp)),
            pl.BlockSpec((2, MOBA_BLOCK, MOBA_BLOCK), lambda bi, hp, qb: (hp, 0, 0)),
            pl.BlockSpec((2, MOBA_BLOCK, MOBA_BLOCK), lambda bi, hp, qb: (hp, 0, 0)),
        ],
        out_specs=pl.BlockSpec((1, MOBA_BLOCK, LANES), lambda bi, hp, qb: (bi, qb, hp)),
        out_shape=jax.ShapeDtypeStruct((b, s, D_MODEL), BF16),
        scratch_shapes=[pltpu.VMEM((2, nblk, MOBA_BLOCK, MOBA_BLOCK), F32),
                        pltpu.VMEM((2, MOBA_BLOCK, MOBA_BLOCK), F32),
                        pltpu.VMEM((2, MOBA_BLOCK, LANES), F32)],
        compiler_params=_params("parallel", "parallel", "arbitrary"),
    )(qkv, _moba_gates(qkv, nblk), qkv, qkv, own_bias, prev_bias)


def _moba_layer(xt, b, s, w_qkv, w_o, own_bias, prev_bias, ln_g, ln_b):
    d = xt.shape[1]
    scale = jnp.concatenate([jnp.full((d,), HEAD_DIM ** -0.5, F32), jnp.ones((2 * d,), F32)])
    qkv = _matmul(xt, (w_qkv * scale).astype(BF16), BF16)
    o = _moba_attention(qkv.reshape(b, s, 3 * d), own_bias, prev_bias)
    return _proj_ln(xt, o.reshape(b * s, d), w_o.astype(BF16), ln_g, ln_b)


def _softplus(u):
    return jnp.maximum(u, 0.0) + jnp.log1p(jnp.exp(-jnp.abs(u)))


def _rwkv_proj_kernel(*refs, seq_blocks, has_vres):
    if has_vres:
        (x_ref, xp_ref, vf_ref, mix_ref, wr_ref, wk_ref, wv_ref, w0_ref, w1_ref, w2_ref, a0_ref, a1_ref, a2_ref,
         g1_ref, g2_ref, v0_ref, v1_ref, v2_ref, r_ref, k_ref, v_ref, lw_ref, a_ref, g_ref) = refs
    else:
        (x_ref, xp_ref, mix_ref, wr_ref, wk_ref, wv_ref, w0_ref, w1_ref, w2_ref, a0_ref, a1_ref, a2_ref,
         g1_ref, g2_ref, r_ref, k_ref, v_ref, lw_ref, a_ref, g_ref) = refs
    x = x_ref[...]
    rows = lax.broadcasted_iota(jnp.int32, x.shape, 0)
    first_in_seq = pl.program_id(0) % seq_blocks == 0
    carry_row = jnp.where(first_in_seq, 0.0, xp_ref[7:8, :])
    x_prev = jnp.where(rows == 0, carry_row, pltpu.roll(x, 1, axis=0))
    xx = x_prev - x

    def mixed(i):
        return (x + xx * mix_ref[i:i + 1, :]).astype(BF16)

    def lora(inp, w_in, w_out, act=None):
        mid = _dot(inp, w_in[...])
        if act is not None:
            mid = act(mid)
        return _dot(mid.astype(BF16), w_out[...])

    r_ref[...] = _dot(mixed(0), wr_ref[...])
    k_ref[...] = _dot(mixed(1), wk_ref[...])
    xv = mixed(2)
    v = _dot(xv, wv_ref[...])
    if has_vres:
        v = v + (vf_ref[...] - v) * jax.nn.sigmoid(v0_ref[...] + lora(xv, v1_ref, v2_ref))
    v_ref[...] = v
    w_log = -_softplus(-(w0_ref[...] + lora(mixed(3), w1_ref, w2_ref, jnp.tanh))) - 0.5
    lw_ref[...] = -jnp.exp(w_log)
    a_ref[...] = jax.nn.sigmoid(a0_ref[...] + lora(mixed(4), a1_ref, a2_ref))
    g_ref[...] = lora(mixed(5), g1_ref, g2_ref, jax.nn.sigmoid)


def _rwkv_proj(xt, seq_len, v_first, mix, w_rkv, w0, w1, w2, a0, a1, a2, g1, g2, vres):
    t, d = xt.shape
    assert seq_len % ROW_BLOCK == 0
    row = lambda i: (i, 0)
    fix = lambda i: (0, 0)
    prev8 = lambda i: (jnp.maximum(i * (ROW_BLOCK // 8) - 1, 0), 0)
    has_vres = vres is not None
    vec = lambda z: z.reshape(1, d)
    bf = lambda z: z.astype(BF16)
    ins = [xt, xt] + ([v_first] if has_vres else []) + [
        mix, bf(w_rkv[0]), bf(w_rkv[1]), bf(w_rkv[2]), vec(w0), bf(w1), bf(w2), vec(a0), bf(a1), bf(a2), bf(g1), bf(g2)]
    if has_vres:
        ins += [vec(vres[0]), bf(vres[1]), bf(vres[2])]
    specs = [pl.BlockSpec((ROW_BLOCK, d), row), pl.BlockSpec((8, d), prev8)]
    if has_vres:
        specs.append(pl.BlockSpec((ROW_BLOCK, d), row))
    specs += [pl.BlockSpec(z.shape, fix) for z in ins[len(specs):]]
    return pl.pallas_call(
        functools.partial(_rwkv_proj_kernel, seq_blocks=seq_len // ROW_BLOCK, has_vres=has_vres),
        grid=(t // ROW_BLOCK,),
        in_specs=specs,
        out_specs=[pl.BlockSpec((ROW_BLOCK, d), row)] * 6,
        out_shape=[jax.ShapeDtypeStruct((t, d), F32)] * 6,
        compiler_params=_params("parallel"),
    )(*ins)


def _wkv_kernel(r_ref, k_ref, v_ref, lw_ref, a_ref, kk_ref, ka_ref, rk_ref, gg_ref, gb_ref, o_ref, s_ref, *, n_chunks):
    c = WKV_CHUNK

    @pl.when(pl.program_id(2) == 0)
    def _reset():
        s_ref[...] = jnp.zeros_like(s_ref)

    lane = lax.broadcasted_iota(jnp.int32, (c, LANES), 1)
    row = lax.broadcasted_iota(jnp.int32, (c, LANES), 0)
    col = jnp.bitwise_and(lane, HEAD_DIM - 1)
    h0 = lane < HEAD_DIM
    sq_r = lax.broadcasted_iota(jnp.int32, (c, c), 0)
    sq_c = lax.broadcasted_iota(jnp.int32, (c, c), 1)
    tri_incl = jnp.where(sq_c <= sq_r, 1.0, 0.0).astype(BF16)
    eye = jnp.where(sq_c == sq_r, 1.0, 0.0)
    st_r = lax.broadcasted_iota(jnp.int32, (LANES, LANES), 0)
    st_c = lax.broadcasted_iota(jnp.int32, (LANES, LANES), 1)
    same_head = (st_r < HEAD_DIM) == (st_c < HEAD_DIM)
    k_k, k_a, r_k, gn_g, gn_b = kk_ref[...], ka_ref[...], rk_ref[...], gg_ref[...], gb_ref[...]

    def head_sum(z):
        s0 = jnp.sum(jnp.where(h0, z, 0.0), axis=-1, keepdims=True)
        s1 = jnp.sum(jnp.where(h0, 0.0, z), axis=-1, keepdims=True)
        return jnp.where(h0, s0, s1)

    chunks = range(n_chunks)
    sl = [slice(ci * c, (ci + 1) * c) for ci in chunks]
    owns = (h0, jnp.logical_not(h0))
    keys = [(ci, h) for ci in chunks for h in range(2)]
    r_all, k_all, v_all, lw_all, a_all = r_ref[...], k_ref[...], v_ref[...], lw_ref[...], a_ref[...]
    lw_parts = _split3(lw_all)
    cum, kh, at, rt, bt_kt, bh_kh, e_tot, v, vv = [], [], [], [], [], [], [], [], []
    for ci in chunks:
        rs = sl[ci]
        cum.append(sum(_dot(tri_incl, part[rs]) for part in lw_parts))
    for ci in chunks:
        rs = sl[ci]
        k, a, lw = k_all[rs], a_all[rs], lw_all[rs]
        kk = k * k_k
        kk = kk / jnp.maximum(jnp.sqrt(head_sum(kk * kk)), 1e-12)
        kh.append(k * (1.0 + (a - 1.0) * k_a))
        bb = kk * a
        l_end = cum[ci][c - 1:c, :]
        e_end = jnp.exp(l_end - cum[ci])
        e_neg = jnp.exp(-cum[ci])
        e_tot.append(jnp.exp(l_end))
        at.append(-kk * jnp.exp(cum[ci] - lw))
        rt.append(r_all[rs] * jnp.exp(cum[ci]))
        bt_kt.append(jnp.concatenate([bb * e_neg, kh[ci] * e_neg], axis=0).astype(BF16))
        bh_kh.append(jnp.concatenate([bb * e_end, kh[ci] * e_end], axis=0).astype(BF16))
        v.append(v_all[rs])
        vv.append(jnp.concatenate([v[ci], v[ci]], axis=0).astype(BF16))
    at_own, a_top, a_bot, pw, inv = {}, {}, {}, {}, {}
    for key in keys:
        ci, h = key
        at_own[key] = jnp.where(owns[h], at[ci], 0.0)
        lhs = jnp.concatenate([at_own[key], jnp.where(owns[h], rt[ci], 0.0)], axis=0).astype(BF16)
        p = _dot_t(lhs, bt_kt[ci])
        a_top[key] = jnp.where(col < row, p[:c, :], 0.0)
        a_bot[key] = jnp.where(col <= row, p[c:, :], 0.0).astype(BF16)
        pw[key] = a_top[key][:, :c]
        inv[key] = eye + pw[key]
    span = 2
    while span < c:
        for key in keys:
            pw_b = pw[key].astype(BF16)
            pw[key] = _dot(pw_b, pw_b)
        for key in keys:
            inv[key] = inv[key] + _dot(pw[key].astype(BF16), inv[key].astype(BF16))
        span *= 2
    inv_b = {key: inv[key].astype(BF16) for key in keys}
    akv = {key: _dot(jnp.where(h0, 0.0, a_top[key]).astype(BF16), vv[key[0]]) for key in keys}
    at2h = {key: _dot(inv_b[key], at_own[key].astype(BF16)) for key in keys}
    uvh = {key: _dot(inv_b[key], akv[key].astype(BF16)) for key in keys}
    at2 = [(at2h[ci, 0] + at2h[ci, 1]).astype(BF16) for ci in chunks]
    uv = [jnp.where(h0, uvh[ci, 0], uvh[ci, 1]) for ci in chunks]
    w_step = [jnp.where(same_head, lax.dot_general(at2[ci], bh_kh[ci][:c], _CONTRACT_FIRST,
                                                   preferred_element_type=F32), 0.0).astype(BF16) for ci in chunks]
    n_step = [jnp.where(same_head, lax.dot_general(jnp.concatenate([uv[ci], v[ci]], axis=0).astype(BF16), bh_kh[ci],
                                                   _CONTRACT_FIRST, preferred_element_type=F32), 0.0) for ci in chunks]
    state = s_ref[...]
    entering = []
    for ci in chunks:
        state_b = state.astype(BF16)
        entering.append(state_b)
        state = state * e_tot[ci] + _dot(state_b, w_step[ci]) + n_step[ci]
    s_ref[...] = state
    u = [_dot_t(at2[ci], entering[ci]) + uv[ci] for ci in chunks]
    y_state = [_dot_t(rt[ci].astype(BF16), entering[ci]) for ci in chunks]
    for ci in chunks:
        u_v = jnp.concatenate([u[ci], v[ci]], axis=0).astype(BF16)
        y = y_state[ci] + jnp.where(h0, _dot(a_bot[ci, 0], u_v), _dot(a_bot[ci, 1], u_v))
        mu = head_sum(y) * (1.0 / HEAD_DIM)
        yc = y - mu
        var = head_sum(yc * yc) * (1.0 / HEAD_DIM)
        yn = yc * lax.rsqrt(var + GN_EPS) * gn_g + gn_b
        o_ref[sl[ci], :] = yn + head_sum(r_all[sl[ci]] * kh[ci] * r_k) * v[ci]


WKV_BLOCK = 512


def _wkv(r, k, v, lw, a, k_k, k_a, r_k, gn_g, gn_b, batch, seq_len):
    t, d = r.shape
    assert seq_len % WKV_BLOCK == 0 and WKV_BLOCK % WKV_CHUNK == 0
    nt = seq_len // WKV_BLOCK
    tok = lambda bi, hp, ti: (bi * nt + ti, hp)
    par = lambda bi, hp, ti: (0, hp)
    vec = lambda z: z.reshape(1, d)
    return pl.pallas_call(
        functools.partial(_wkv_kernel, n_chunks=WKV_BLOCK // WKV_CHUNK),
        grid=(batch, N_PAIRS, nt),
        in_specs=[pl.BlockSpec((WKV_BLOCK, LANES), tok)] * 5 + [pl.BlockSpec((1, LANES), par)] * 5,
        out_specs=pl.BlockSpec((WKV_BLOCK, LANES), tok),
        out_shape=jax.ShapeDtypeStruct((t, d), F32),
        scratch_shapes=[pltpu.VMEM((LANES, LANES), F32)],
        compiler_params=_params("parallel", "parallel", "arbitrary"),
    )(r, k, v, lw, a, vec(k_k), vec(k_a), vec(r_k), vec(gn_g), vec(gn_b))


def _rwkv_layer(xt, b, s, v_first, mix, w_rkv, w0, w1, w2, a0, a1, a2, g1, g2, k_k, k_a, r_k, gn_g, gn_b, w_o, vres,
                ln_g, ln_b):
    r, k, v, lw, a, g = _rwkv_proj(xt, s, v_first, mix, w_rkv, w0, w1, w2, a0, a1, a2, g1, g2, vres)
    y = _wkv(r, k, v, lw, a, k_k, k_a, r_k, gn_g, gn_b, b, s)
    return _proj_ln(xt, y, w_o.astype(BF16), ln_g, ln_b, mult=g), (v if vres is None else v_first)


def _route_kernel(x_ref, w_ref, b_ref, pos_ref, gate_ref, off_ref, cnt_ref, *, tt):
    xh, xm, _ = _split3(x_ref[...])
    w = w_ref[...]
    wh = w.astype(BF16)
    wm = (w - wh.astype(F32)).astype(BF16)
    scores = jax.nn.sigmoid(_dot_t(wh, xh) + _dot_t(wh, xm) + _dot_t(wm, xh))
    biased = scores + b_ref[...]
    per = N_EXPERTS // N_GROUPS
    neg = -jnp.inf
    row_g = lax.broadcasted_iota(jnp.int32, (per, tt), 0)
    group_scores = []
    for g in range(N_GROUPS):
        grp = biased[g * per:(g + 1) * per, :]
        m1, i1 = _first_max(grp, row_g, per)
        m2 = jnp.max(jnp.where(row_g == i1, neg, grp), axis=0, keepdims=True)
        group_scores.append(m1 + m2)
    gsc = jnp.concatenate(group_scores, axis=0)
    row_n = lax.broadcasted_iota(jnp.int32, (N_GROUPS, tt), 0)
    keep = jnp.zeros((N_GROUPS, tt), F32)
    for _ in range(TOPK_GROUPS):
        _, ix = _first_max(gsc, row_n, N_GROUPS)
        pick = row_n == ix
        keep = jnp.where(pick, 1.0, keep)
        gsc = jnp.where(pick, neg, gsc)
    keep_e = jnp.concatenate([jnp.broadcast_to(keep[g:g + 1, :], (per, tt)) for g in range(N_GROUPS)], axis=0)
    cand = jnp.where(keep_e > 0.0, biased, neg)
    row_e = lax.broadcasted_iota(jnp.int32, (N_EXPERTS, tt), 0)
    picks = []
    for _ in range(TOP_K):
        _, ix = _first_max(cand, row_e, N_EXPERTS)
        pick = row_e == ix
        picks.append(pick)
        cand = jnp.where(pick, neg, cand)
    raw = [jnp.sum(jnp.where(p, scores, 0.0), axis=0, keepdims=True) for p in picks]
    total = raw[0]
    for r in raw[1:]:
        total = total + r
    gate_ref[0] = jnp.concatenate([r / total * ROUTED_SCALE for r in raw], axis=0)

    sel = jnp.zeros((N_EXPERTS, tt), F32)
    for p in picks:
        sel = jnp.where(p, 1.0, sel)
    cb = 256
    cr = lax.broadcasted_iota(jnp.int32, (cb, cb), 0)
    cc = lax.broadcasted_iota(jnp.int32, (cb, cb), 1)
    upper_incl = jnp.where(cr <= cc, 1.0, 0.0).astype(BF16)
    carry = jnp.zeros((N_EXPERTS, 1), F32)
    ranks = []
    for c0 in range(0, tt, cb):
        blk = sel[:, c0:c0 + cb]
        incl = _dot(blk.astype(BF16), upper_incl)
        ranks.append(incl - blk + carry)
        carry = carry + incl[:, cb - 1:cb]
    rank = jnp.concatenate(ranks, axis=1)
    count = jnp.broadcast_to(carry, (N_EXPERTS, LANES))
    aligned8 = jnp.floor((count + 7.0) * 0.125)
    er = lax.broadcasted_iota(jnp.int32, (N_EXPERTS, N_EXPERTS), 0)
    ec = lax.broadcasted_iota(jnp.int32, (N_EXPERTS, N_EXPERTS), 1)
    strict_lower = jnp.where(ec < er, 1.0, 0.0).astype(BF16)
    start = 8.0 * _dot(strict_lower, aligned8.astype(BF16))
    where_to = (start[:, :1] + rank) * float(D_MODEL // LANES)
    pos_ref[0] = jnp.concatenate(
        [jnp.sum(jnp.where(p, where_to, 0.0), axis=0, keepdims=True) for p in picks], axis=0).astype(jnp.int32)
    off_ref[0] = start.astype(jnp.int32)
    cnt_ref[0] = count.astype(jnp.int32)


MOE_TILE = 1024
MOE_CHUNK = 160
EXPERTS_PER_STEP = 2
SEG_ALIGN = 8
ROW_LOOP_UNROLL = 4


def _route_tiles(xt, w_router, router_bias):
    t, d = xt.shape
    tt = MOE_TILE
    n_tiles = t // tt
    tile3 = lambda i: (i, 0, 0)
    pos, gate, off, cnt = pl.pallas_call(
        functools.partial(_route_kernel, tt=tt),
        grid=(n_tiles,),
        in_specs=[pl.BlockSpec((tt, d), lambda i: (i, 0)),
                  pl.BlockSpec((N_EXPERTS, d), lambda i: (0, 0)),
                  pl.BlockSpec((N_EXPERTS, 1), lambda i: (0, 0))],
        out_specs=[pl.BlockSpec((1, TOP_K, tt), tile3), pl.BlockSpec((1, TOP_K, tt), tile3),
                   pl.BlockSpec((1, N_EXPERTS, LANES), tile3), pl.BlockSpec((1, N_EXPERTS, LANES), tile3)],
        out_shape=[jax.ShapeDtypeStruct((n_tiles, TOP_K, tt), jnp.int32),
                   jax.ShapeDtypeStruct((n_tiles, TOP_K, tt), F32),
                   jax.ShapeDtypeStruct((n_tiles, N_EXPERTS, LANES), jnp.int32),
                   jax.ShapeDtypeStruct((n_tiles, N_EXPERTS, LANES), jnp.int32)],
        compiler_params=_params("parallel"),
    )(xt, w_router.astype(F32).T, router_bias.astype(F32).reshape(N_EXPERTS, 1))
    token_major = lambda z: z.transpose(0, 2, 1).reshape(n_tiles, tt * TOP_K)
    return token_major(pos), token_major(gate), off[:, :, 0].reshape(-1), cnt[:, :, 0].reshape(-1)


def _moe_tile_kernel(off_ref, cnt_ref, nch_ref, x_hbm, pos_hbm, gate_hbm, wg_ref, wu_ref, wd_ref, o_hbm,
                     stage, tok, pos_s, gate_s, sems, *, tt):
    ti = pl.program_id(0)
    step = pl.program_id(1)
    sub = D_MODEL // LANES
    rows = MOE_CHUNK

    def tile_copy(src, dst, sem):
        return pltpu.make_async_copy(src, dst, sem)

    @pl.when(step == 0)
    def _distribute():
        copies = [tile_copy(x_hbm.at[pl.ds(pl.multiple_of(ti * tt * sub, tt * sub), tt * sub)], tok, sems.at[0]),
                  tile_copy(pos_hbm.at[ti], pos_s, sems.at[1]),
                  tile_copy(gate_hbm.at[ti], gate_s, sems.at[2])]
        for c in copies:
            c.start()
        for c in copies:
            c.wait()
        zero_group = jnp.zeros((SEG_ALIGN * sub, LANES), F32)

        def clear_tail(ex, carry):
            end = off_ref[ti * N_EXPERTS + ex] + cnt_ref[ti * N_EXPERTS + ex]
            g0 = pl.multiple_of(lax.shift_left(lax.shift_right_logical(end, 3), 3) * sub, SEG_ALIGN * sub)
            stage[pl.ds(g0, SEG_ALIGN * sub), :] = zero_group
            return carry

        lax.fori_loop(0, N_EXPERTS, clear_tail, 0)
        last = ti * N_EXPERTS + N_EXPERTS - 1
        total = off_ref[last] + lax.shift_left(lax.shift_right_logical(cnt_ref[last] + 7, 3), 3)
        t0 = pl.multiple_of(total * sub, SEG_ALIGN * sub)
        stage[pl.ds(t0, rows * sub), :] = jnp.zeros((rows * sub, LANES), F32)

        def place(tq, carry):
            for t in [ROW_LOOP_UNROLL * tq + u for u in range(ROW_LOOP_UNROLL)]:
                row = tok[pl.ds(pl.multiple_of(t * sub, sub), sub), :]
                for j in range(TOP_K):
                    stage[pl.ds(pl.multiple_of(pos_s[t * TOP_K + j], sub), sub), :] = row
            return carry

        lax.fori_loop(0, tt // ROW_LOOP_UNROLL, place, 0)

    row_id = lax.broadcasted_iota(jnp.int32, (rows, LANES), 0)
    experts = []
    for i in range(EXPERTS_PER_STEP):
        e = (ti * (N_EXPERTS // EXPERTS_PER_STEP) + step) * EXPERTS_PER_STEP + i
        experts.append((i, off_ref[e], cnt_ref[e], nch_ref[e]))
    n_steps = experts[0][3]
    for ex in experts[1:]:
        n_steps = jnp.maximum(n_steps, ex[3])

    def ffn_step(c, carry):
        loaded = []
        for i, off, cnt, n_mine in experts:
            c_mine = jnp.minimum(c, jnp.maximum(n_mine - 1, 0))
            base = pl.multiple_of((off + c_mine * rows) * sub, SEG_ALIGN * sub)
            loaded.append((base, [stage[pl.ds(base + k, rows, stride=sub), :] for k in range(sub)]))
        outs = []
        for (i, off, cnt, n_mine), (base, parts) in zip(experts, loaded):
            xb = jnp.concatenate(parts, axis=1).astype(BF16)
            h = jax.nn.silu(_dot(xb, wg_ref[i])) * _dot(xb, wu_ref[i])
            outs.append(_dot(h.astype(BF16), wd_ref[i]))
        for (i, off, cnt, n_mine), (base, parts), y in zip(experts, loaded, outs):
            live = row_id < cnt - c * rows
            for k in range(sub):
                stage[pl.ds(base + k, rows, stride=sub), :] = jnp.where(live, y[:, k * LANES:(k + 1) * LANES], parts[k])
        return carry

    lax.fori_loop(0, n_steps, ffn_step, 0)

    @pl.when(step == N_EXPERTS // EXPERTS_PER_STEP - 1)
    def _combine():
        def gather(tq, carry):
            for t in [ROW_LOOP_UNROLL * tq + u for u in range(ROW_LOOP_UNROLL)]:
                acc = jnp.zeros((sub, LANES), F32)
                for j in range(TOP_K):
                    p = pl.multiple_of(pos_s[t * TOP_K + j], sub)
                    acc = acc + gate_s[t * TOP_K + j] * stage[pl.ds(p, sub), :]
                tok[pl.ds(pl.multiple_of(t * sub, sub), sub), :] = acc
            return carry

        lax.fori_loop(0, tt // ROW_LOOP_UNROLL, gather, 0)
        out = tile_copy(tok, o_hbm.at[pl.ds(pl.multiple_of(ti * tt * sub, tt * sub), tt * sub)], sems.at[0])
        out.start()
        out.wait()


def _routed_experts(x8, pos, gate, off, cnt, w_gate, w_up, w_down):
    sub = SUBROWS
    t, d = x8.shape[0] // sub, D_MODEL
    tt = MOE_TILE
    n_tiles = t // tt
    stage_rows = TOP_K * tt + N_EXPERTS * SEG_ALIGN + MOE_CHUNK
    any_spec = pl.BlockSpec(memory_space=pl.ANY)
    grid_spec = pltpu.PrefetchScalarGridSpec(
        num_scalar_prefetch=3,
        grid=(n_tiles, N_EXPERTS // EXPERTS_PER_STEP),
        in_specs=[any_spec, any_spec, any_spec,
                  pl.BlockSpec((EXPERTS_PER_STEP, d, D_EXPERT), lambda ti, e, *_: (e, 0, 0)),
                  pl.BlockSpec((EXPERTS_PER_STEP, d, D_EXPERT), lambda ti, e, *_: (e, 0, 0)),
                  pl.BlockSpec((EXPERTS_PER_STEP, D_EXPERT, d), lambda ti, e, *_: (e, 0, 0))],
        out_specs=any_spec,
        scratch_shapes=[pltpu.VMEM((stage_rows * sub, LANES), F32),
                        pltpu.VMEM((tt * sub, LANES), F32),
                        pltpu.SMEM((TOP_K * tt,), jnp.int32),
                        pltpu.SMEM((TOP_K * tt,), F32),
                        pltpu.SemaphoreType.DMA((3,))],
    )
    return pl.pallas_call(
        functools.partial(_moe_tile_kernel, tt=tt),
        grid_spec=grid_spec,
        out_shape=jax.ShapeDtypeStruct((t * sub, LANES), F32),
        compiler_params=_params("arbitrary", "arbitrary"),
    )(off, cnt, (cnt + MOE_CHUNK - 1) // MOE_CHUNK, x8, pos, gate, w_gate, w_up, w_down)


def _shared_ln_kernel(x_ref, r8_ref, sg_ref, su_ref, sd_ref, g_ref, b_ref, o_ref):
    x = x_ref[...]
    xb = x.astype(BF16)
    h = jax.nn.silu(_dot(xb, sg_ref[...])) * _dot(xb, su_ref[...])
    f = _load_row_major(r8_ref, x.shape[0]) + _dot(h.astype(BF16), sd_ref[...])
    o_ref[...] = _layer_norm_rows(ALPHA * x + f, g_ref[...], b_ref[...])


def _shared_ln(xt, routed8, sh_gate, sh_up, sh_down, g, b):
    t, d = xt.shape
    row = lambda i: (i, 0)
    fix = lambda i: (0, 0)
    return pl.pallas_call(
        _shared_ln_kernel,
        grid=(t // ROW_BLOCK,),
        in_specs=[pl.BlockSpec((ROW_BLOCK, d), row), pl.BlockSpec((ROW_BLOCK * SUBROWS, LANES), row),
                  pl.BlockSpec((d, D_EXPERT), fix), pl.BlockSpec((d, D_EXPERT), fix), pl.BlockSpec((D_EXPERT, d), fix),
                  pl.BlockSpec((1, d), fix), pl.BlockSpec((1, d), fix)],
        out_specs=pl.BlockSpec((ROW_BLOCK, d), row),
        out_shape=jax.ShapeDtypeStruct((t, d), F32),
        compiler_params=_params("parallel"),
    )(xt, routed8, sh_gate.astype(BF16), sh_up.astype(BF16), sh_down.astype(BF16), g.reshape(1, d), b.reshape(1, d))


def _moe_layer(xt, x8, w_router, router_bias, w_gate, w_up, w_down, sh_gate, sh_up, sh_down, ln_g, ln_b):
    assert xt.shape[0] % MOE_TILE == 0
    pos, gate, off, cnt = _route_tiles(xt, w_router, router_bias)
    routed8 = _routed_experts(x8, pos, gate, off, cnt, w_gate.astype(BF16), w_up.astype(BF16), w_down.astype(BF16))
    return _shared_ln(xt, routed8, sh_gate, sh_up, sh_down, ln_g, ln_b)


def kernel(x, moba_w_qkv, moba_w_o, rel_bias, rwkv_mix, rwkv_w_rkv, rwkv_w0, rwkv_w1, rwkv_w2,
           rwkv_a0, rwkv_a1, rwkv_a2, rwkv_v0, rwkv_v1, rwkv_v2, rwkv_g1, rwkv_g2, rwkv_k_k,
           rwkv_k_a, rwkv_r_k, rwkv_gn_g, rwkv_gn_b, rwkv_w_o, moe_w_router, moe_router_bias,
           moe_w_gate, moe_w_up, moe_w_down, moe_sh_gate, moe_sh_up, moe_sh_down,
           ln_mix_g, ln_mix_b, ln_ffn_g, ln_ffn_b):
    assert x.shape[-1] == D_MODEL
    own_bias, prev_bias = _moba_bias_tables(rel_bias)
    b, s, d = x.shape
    xt = x.reshape(b * s, d)
    v_first = None
    for i in range(DEPTH):
        j = i // 2
        if i % 2 == 0:
            xt, x8 = _moba_layer(xt, b, s, moba_w_qkv[j], moba_w_o[j], own_bias, prev_bias, ln_mix_g[i], ln_mix_b[i])
        else:
            vres = None if j == 0 else (rwkv_v0[j - 1], rwkv_v1[j - 1], rwkv_v2[j - 1])
            (xt, x8), v_first = _rwkv_layer(xt, b, s, v_first, rwkv_mix[j], rwkv_w_rkv[j], rwkv_w0[j], rwkv_w1[j],
                                            rwkv_w2[j], rwkv_a0[j], rwkv_a1[j], rwkv_a2[j], rwkv_g1[j],
                                            rwkv_g2[j], rwkv_k_k[j], rwkv_k_a[j], rwkv_r_k[j],
                                            rwkv_gn_g[j], rwkv_gn_b[j], rwkv_w_o[j], vres, ln_mix_g[i], ln_mix_b[i])
        xt = _moe_layer(xt, x8, moe_w_router[i], moe_router_bias[i], moe_w_gate[i], moe_w_up[i], moe_w_down[i],
                        moe_sh_gate[i], moe_sh_up[i], moe_sh_down[i], ln_ffn_g[i], ln_ffn_b[i])
    return xt.reshape(b, s, d)
```

```python
import functools
import math

import jax
import jax.numpy as jnp
import numpy as np
from jax import lax
from jax.experimental import pallas as pl
from jax.experimental.pallas import tpu as pltpu

F32 = jnp.float32
BF16 = jnp.bfloat16

D_MODEL = 1024
DEPTH = 4
ALPHA = (2 * DEPTH) ** 0.25
LN_EPS = 1e-5

HEAD_DIM = 64
LANES = 128
N_PAIRS = D_MODEL // LANES
MOBA_BLOCK = 256
MOBA_TOPK = 3
N_BUCKETS = 32
MAX_DISTANCE = 128
MASKED = -1e30

GN_EPS = 64e-5
WKV_CHUNK = 64

N_EXPERTS = 64
N_GROUPS = 8
TOPK_GROUPS = 4
TOP_K = 8
D_EXPERT = 256
ROUTED_SCALE = 2.5

VMEM_LIMIT = 56 * 1024 * 1024
ROW_BLOCK = 512

_CONTRACT_LAST = (((1,), (1,)), ((), ()))
_CONTRACT_FIRST = (((0,), (0,)), ((), ()))


def _params(*sem):
    return pltpu.CompilerParams(dimension_semantics=sem, vmem_limit_bytes=VMEM_LIMIT)


def _dot(a, b):
    return jnp.dot(a, b, preferred_element_type=F32)


def _dot_t(a, b):
    return lax.dot_general(a, b, _CONTRACT_LAST, preferred_element_type=F32)


def _split3(x):
    hi = x.astype(BF16)
    r1 = x - hi.astype(F32)
    mid = r1.astype(BF16)
    lo = (r1 - mid.astype(F32)).astype(BF16)
    return hi, mid, lo


def _first_max(vals, idx, big):
    mx = jnp.max(vals, axis=0, keepdims=True)
    return mx, jnp.min(jnp.where(vals == mx, idx, big), axis=0, keepdims=True)


def _layer_norm_rows(y, g, b):
    mu = jnp.mean(y, axis=-1, keepdims=True)
    yc = y - mu
    var = jnp.mean(yc * yc, axis=-1, keepdims=True)
    return yc * lax.rsqrt(var + LN_EPS) * g + b


def _mm_kernel(x_ref, w_ref, o_ref):
    x = x_ref[...].astype(BF16)
    o_ref[...] = _dot(x, w_ref[...]).astype(o_ref.dtype)


def _matmul(x, w, out_dtype):
    m, k = x.shape
    n = w.shape[1]
    return pl.pallas_call(
        _mm_kernel,
        grid=(m // ROW_BLOCK,),
        in_specs=[pl.BlockSpec((ROW_BLOCK, k), lambda i: (i, 0)),
                  pl.BlockSpec((k, n), lambda i: (0, 0))],
        out_specs=pl.BlockSpec((ROW_BLOCK, n), lambda i: (i, 0)),
        out_shape=jax.ShapeDtypeStruct((m, n), out_dtype),
        compiler_params=_params("parallel"),
    )(x, w)


SUBROWS = D_MODEL // LANES


def _store_row_major(o8_ref, y):
    for k in range(SUBROWS):
        o8_ref[pl.ds(k, y.shape[0], stride=SUBROWS), :] = y[:, k * LANES:(k + 1) * LANES]


def _load_row_major(r8_ref, rows):
    return jnp.concatenate([r8_ref[pl.ds(k, rows, stride=SUBROWS), :] for k in range(SUBROWS)], axis=1)


def _proj_ln_kernel(x_ref, a_ref, w_ref, g_ref, b_ref, o_ref, o8_ref):
    h = _dot(a_ref[...].astype(BF16), w_ref[...])
    y = _layer_norm_rows(ALPHA * x_ref[...] + h, g_ref[...], b_ref[...])
    o_ref[...] = y
    _store_row_major(o8_ref, y)


def _proj_gate_ln_kernel(x_ref, a_ref, m_ref, w_ref, g_ref, b_ref, o_ref, o8_ref):
    h = _dot((a_ref[...] * m_ref[...]).astype(BF16), w_ref[...])
    y = _layer_norm_rows(ALPHA * x_ref[...] + h, g_ref[...], b_ref[...])
    o_ref[...] = y
    _store_row_major(o8_ref, y)


def _proj_ln(x, a, w, g, b, mult=None):
    t, d = x.shape
    k = a.shape[1]
    row = lambda i: (i, 0)
    fix = lambda i: (0, 0)
    acts = [a] if mult is None else [a, mult]
    kern = _proj_ln_kernel if mult is None else _proj_gate_ln_kernel
    return pl.pallas_call(
        kern,
        grid=(t // ROW_BLOCK,),
        in_specs=[pl.BlockSpec((ROW_BLOCK, d), row)]
        + [pl.BlockSpec((ROW_BLOCK, k), row) for _ in acts]
        + [pl.BlockSpec((k, d), fix), pl.BlockSpec((1, d), fix), pl.BlockSpec((1, d), fix)],
        out_specs=[pl.BlockSpec((ROW_BLOCK, d), row), pl.BlockSpec((ROW_BLOCK * SUBROWS, LANES), row)],
        out_shape=[jax.ShapeDtypeStruct((t, d), F32), jax.ShapeDtypeStruct((t * SUBROWS, LANES), F32)],
        compiler_params=_params("parallel"),
    )(x, *acts, w, g.reshape(1, d), b.reshape(1, d))


def _rel_bucket_np(dist):
    n = np.maximum(dist, 0)
    max_exact = N_BUCKETS // 2
    nf = np.maximum(n, 1).astype(np.float32)
    large = max_exact + (np.log(nf / np.float32(max_exact)) / np.float32(math.log(MAX_DISTANCE / max_exact))
                         * np.float32(N_BUCKETS - max_exact)).astype(np.int32)
    return np.where(n < max_exact, n, np.minimum(large, N_BUCKETS - 1))


def _moba_bias_tables(rel_bias):
    off = np.arange(MOBA_BLOCK)
    rel = off[:, None] - off[None, :]
    assert int(_rel_bucket_np(np.array([MOBA_BLOCK + 1]))[0]) == N_BUCKETS - 1
    shifted = (rel_bias - rel_bias[N_BUCKETS - 1:]).astype(F32)

    def lookup(buckets):
        onehot = (jnp.asarray(buckets.reshape(-1, 1)) == jnp.arange(N_BUCKETS)[None, :]).astype(F32)
        table = jnp.dot(onehot, shifted, precision=lax.Precision.HIGHEST)
        return table.T.reshape(-1, MOBA_BLOCK, MOBA_BLOCK)

    own = jnp.where(jnp.asarray(rel >= 0)[None], lookup(_rel_bucket_np(rel)), MASKED)
    return own, lookup(_rel_bucket_np(rel + MOBA_BLOCK))


def _grouped_loop(n, body):
    def four(i, carry):
        for u in range(4):
            body(4 * i + u)
        return carry

    lax.fori_loop(0, lax.shift_right_logical(n, 2), four, 0)
    done = jnp.bitwise_and(n, -4)

    @pl.when(jnp.bitwise_and(n, 2) == 2)
    def _two_more():
        body(done)
        body(done + 1)

    @pl.when(jnp.bitwise_and(n, 1) == 1)
    def _last():
        body(n - 1)


def _moba_gate_kernel(q_ref, k_ref, g_ref, *, nblk):
    blk = MOBA_BLOCK
    nbp = -(-nblk // 8) * 8
    s = nblk * blk
    rows = [jnp.sum(k_ref[0, j * blk:(j + 1) * blk, :].astype(F32), axis=0, keepdims=True) * (1.0 / blk)
            for j in range(nblk)]
    if nbp > nblk:
        rows.append(jnp.zeros((nbp - nblk, LANES), F32))
    km = jnp.concatenate(rows, axis=0)
    lane_k = lax.broadcasted_iota(jnp.int32, (nbp, LANES), 1)
    q = q_ref[0]
    blk_id = lax.broadcasted_iota(jnp.int32, (nbp, s), 0)
    q_blk = lax.shift_right_logical(lax.broadcasted_iota(jnp.int32, (nbp, s), 1), blk.bit_length() - 1)
    valid = blk_id < q_blk
    src_blk = lax.broadcasted_iota(jnp.int32, (nbp, LANES), 0)
    total = None
    for h in range(2):
        kmf = jnp.where((lane_k < HEAD_DIM) if h == 0 else (lane_k >= HEAD_DIM), km, 0.0)
        kmh = kmf.astype(BF16)
        kml = (kmf - kmh.astype(F32)).astype(BF16)
        gm = jnp.where(valid, _dot_t(kmh, q) + _dot_t(kml, q), -jnp.inf)
        hidden = jnp.where(valid, 1.0, 0.0)
        for _ in range(MOBA_TOPK):
            mx, ix = _first_max(gm, blk_id, nbp)
            pick = (blk_id == ix) & (mx > -jnp.inf)
            hidden = jnp.where(pick, 0.0, hidden)
            gm = jnp.where(pick, -jnp.inf, gm)
        goff = HEAD_DIM if h == 0 else 0
        place = jnp.where(lane_k == src_blk + goff, MASKED, 0.0).astype(BF16)
        gsel = lax.dot_general(hidden.astype(BF16), place, _CONTRACT_FIRST, preferred_element_type=F32)
        total = gsel if total is None else total + gsel
    g_ref[0] = total.astype(g_ref.dtype)


def _moba_gates(qkv, nblk):
    b, s, _ = qkv.shape
    return pl.pallas_call(
        functools.partial(_moba_gate_kernel, nblk=nblk),
        grid=(b, N_PAIRS),
        in_specs=[pl.BlockSpec((1, s, LANES), lambda bi, hp: (bi, 0, hp)),
                  pl.BlockSpec((1, s, LANES), lambda bi, hp: (bi, 0, N_PAIRS + hp))],
        out_specs=pl.BlockSpec((1, s, LANES), lambda bi, hp: (bi, 0, hp)),
        out_shape=jax.ShapeDtypeStruct((b, s, D_MODEL), BF16),
        compiler_params=_params("parallel", "parallel"),
    )(qkv, qkv)


def _moba_kernel(q_ref, g_ref, k_ref, v_ref, own_ref, prev_ref, o_ref, s_ref, mx_ref, acc_ref):
    qb = pl.program_id(2)
    blk = MOBA_BLOCK
    lane = lax.broadcasted_iota(jnp.int32, (blk, LANES), 1)
    q = q_ref[0]
    g = g_ref[0]
    heads = []
    for h in range(2):
        own = (lane < HEAD_DIM) if h == 0 else (lane >= HEAD_DIM)
        goff = HEAD_DIM if h == 0 else 0
        q_aug = jnp.where(own, q, g)
        q_own = jnp.where(own, q, jnp.zeros_like(q))
        heads.append((own, goff, q_aug, q_own))

    def block_rows(j):
        return pl.ds(pl.multiple_of(j * blk, blk), blk)

    kj = k_ref[0, block_rows(qb), :]
    for h, (own, goff, q_aug, q_own) in enumerate(heads):
        s = _dot_t(q_own, kj) + own_ref[h]
        s_ref[h, qb] = s
        mx_ref[h] = s

    def past_logits(j, with_prev_bias):
        kj = k_ref[0, block_rows(j), :]
        for h, (own, goff, q_aug, q_own) in enumerate(heads):
            ej = jnp.where(lane == goff + j, 1.0, 0.0).astype(BF16)
            s = _dot_t(q_aug, jnp.where(own, kj, ej))
            if with_prev_bias:
                s = s + prev_ref[h]
            s_ref[h, j] = s
            mx_ref[h] = jnp.maximum(mx_ref[h], s)

    @pl.when(qb >= 1)
    def _prev_block():
        past_logits(qb - 1, True)

    _grouped_loop(jnp.maximum(qb - 1, 0), lambda j: past_logits(j, False))

    for h in range(2):
        mm = mx_ref[h]
        row_max = jnp.max(jnp.maximum(mm[:, :LANES], mm[:, LANES:]), axis=-1, keepdims=True)
        mx_ref[h] = jnp.broadcast_to(row_max, (blk, blk))
    acc_ref[...] = jnp.zeros_like(acc_ref)

    def accumulate(j):
        vj = v_ref[0, block_rows(j), :]
        for h, (own, goff, q_aug, q_own) in enumerate(heads):
            p = jnp.exp(s_ref[h, j] - mx_ref[h])
            acc_ref[h] = acc_ref[h] + _dot(p.astype(BF16), jnp.where(own, vj, jnp.ones_like(vj)))

    _grouped_loop(qb + 1, accumulate)

    a0 = acc_ref[0]
    a1 = acc_ref[1]
    o0 = a0 / pltpu.roll(a0, HEAD_DIM, axis=1)
    o1 = a1 / pltpu.roll(a1, HEAD_DIM, axis=1)
    o_ref[0] = jnp.where(lane < HEAD_DIM, o0, o1).astype(o_ref.dtype)


def _moba_attention(qkv, own_bias, prev_bias):
    b, s, _ = qkv.shape
    assert s % MOBA_BLOCK == 0
    nblk = s // MOBA_BLOCK
    assert nblk <= HEAD_DIM
    return pl.pallas_call(
        _moba_kernel,
        grid=(b, N_PAIRS, nblk),
        in_specs=[
            pl.BlockSpec((1, MOBA_BLOCK, LANES), lambda bi, hp, qb: (bi, qb, hp)),
            pl.BlockSpec((1, MOBA_BLOCK, LANES), lambda bi, hp, qb: (bi, qb, hp)),
            pl.BlockSpec((1, s, LANES), lambda bi, hp, qb: (bi, 0, N_PAIRS + hp)),
            pl.BlockSpec((1, s, LANES), lambda bi, hp, qb: (bi, 0, 2 * N_PAIRS + hp)),
            pl.BlockSpec((2, MOBA_BLOCK, MOBA_BLOCK), lambda bi, hp, qb: (hp, 0, 0)),
            pl.BlockSpec((2, MOBA_BLOCK, MOBA_BLOCK), lambda bi, hp, qb: (hp, 0, 0)),
        ],
        out_specs=pl.BlockSpec((1, MOBA_BLOCK, LANES), lambda bi, hp, qb: (bi, qb, hp)),
        out_shape=jax.ShapeDtypeStruct((b, s, D_MODEL), BF16),
        scratch_shapes=[pltpu.VMEM((2, nblk, MOBA_BLOCK, MOBA_BLOCK), F32),
                        pltpu.VMEM((2, MOBA_BLOCK, MOBA_BLOCK), F32),
                        pltpu.VMEM((2, MOBA_BLOCK, LANES), F32)],
        compiler_params=_params("parallel", "parallel", "arbitrary"),
    )(qkv, _moba_gates(qkv, nblk), qkv, qkv, own_bias, prev_bias)


def _moba_layer(xt, b, s, w_qkv, w_o, own_bias, prev_bias, ln_g, ln_b):
    d = xt.shape[1]
    scale = jnp.concatenate([jnp.full((d,), HEAD_DIM ** -0.5, F32), jnp.ones((2 * d,), F32)])
    qkv = _matmul(xt, (w_qkv * scale).astype(BF16), BF16)
    o = _moba_attention(qkv.reshape(b, s, 3 * d), own_bias, prev_bias)
    return _proj_ln(xt, o.reshape(b * s, d), w_o.astype(BF16), ln_g, ln_b)


def _softplus(u):
    return jnp.maximum(u, 0.0) + jnp.log1p(jnp.exp(-jnp.abs(u)))


def _rwkv_proj_kernel(*refs, seq_blocks, has_vres):
    if has_vres:
        (x_ref, xp_ref, vf_ref, mix_ref, wr_ref, wk_ref, wv_ref, w0_ref, w1_ref, w2_ref, a0_ref, a1_ref, a2_ref,
         g1_ref, g2_ref, v0_ref, v1_ref, v2_ref, r_ref, k_ref, v_ref, lw_ref, a_ref, g_ref) = refs
    else:
        (x_ref, xp_ref, mix_ref, wr_ref, wk_ref, wv_ref, w0_ref, w1_ref, w2_ref, a0_ref, a1_ref, a2_ref,
         g1_ref, g2_ref, r_ref, k_ref, v_ref, lw_ref, a_ref, g_ref) = refs
    x = x_ref[...]
    rows = lax.broadcasted_iota(jnp.int32, x.shape, 0)
    first_in_seq = pl.program_id(0) % seq_blocks == 0
    carry_row = jnp.where(first_in_seq, 0.0, xp_ref[7:8, :])
    x_prev = jnp.where(rows == 0, carry_row, pltpu.roll(x, 1, axis=0))
    xx = x_prev - x

    def mixed(i):
        return (x + xx * mix_ref[i:i + 1, :]).astype(BF16)

    def lora(inp, w_in, w_out, act=None):
        mid = _dot(inp, w_in[...])
        if act is not None:
            mid = act(mid)
        return _dot(mid.astype(BF16), w_out[...])

    r_ref[...] = _dot(mixed(0), wr_ref[...])
    k_ref[...] = _dot(mixed(1), wk_ref[...])
    xv = mixed(2)
    v = _dot(xv, wv_ref[...])
    if has_vres:
        v = v + (vf_ref[...] - v) * jax.nn.sigmoid(v0_ref[...] + lora(xv, v1_ref, v2_ref))
    v_ref[...] = v
    w_log = -_softplus(-(w0_ref[...] + lora(mixed(3), w1_ref, w2_ref, jnp.tanh))) - 0.5
    lw_ref[...] = -jnp.exp(w_log)
    a_ref[...] = jax.nn.sigmoid(a0_ref[...] + lora(mixed(4), a1_ref, a2_ref))
    g_ref[...] = lora(mixed(5), g1_ref, g2_ref, jax.nn.sigmoid)


def _rwkv_proj(xt, seq_len, v_first, mix, w_rkv, w0, w1, w2, a0, a1, a2, g1, g2, vres):
    t, d = xt.shape
    assert seq_len % ROW_BLOCK == 0
    row = lambda i: (i, 0)
    fix = lambda i: (0, 0)
    prev8 = lambda i: (jnp.maximum(i * (ROW_BLOCK // 8) - 1, 0), 0)
    has_vres = vres is not None
    vec = lambda z: z.reshape(1, d)
    bf = lambda z: z.astype(BF16)
    ins = [xt, xt] + ([v_first] if has_vres else []) + [
        mix, bf(w_rkv[0]), bf(w_rkv[1]), bf(w_rkv[2]), vec(w0), bf(w1), bf(w2), vec(a0), bf(a1), bf(a2), bf(g1), bf(g2)]
    if has_vres:
        ins += [vec(vres[0]), bf(vres[1]), bf(vres[2])]
    specs = [pl.BlockSpec((ROW_BLOCK, d), row), pl.BlockSpec((8, d), prev8)]
    if has_vres:
        specs.append(pl.BlockSpec((ROW_BLOCK, d), row))
    specs += [pl.BlockSpec(z.shape, fix) for z in ins[len(specs):]]
    return pl.pallas_call(
        functools.partial(_rwkv_proj_kernel, seq_blocks=seq_len // ROW_BLOCK, has_vres=has_vres),
        grid=(t // ROW_BLOCK,),
        in_specs=specs,
        out_specs=[pl.BlockSpec((ROW_BLOCK, d), row)] * 6,
        out_shape=[jax.ShapeDtypeStruct((t, d), F32)] * 6,
        compiler_params=_params("parallel"),
    )(*ins)


def _wkv_kernel(r_ref, k_ref, v_ref, lw_ref, a_ref, kk_ref, ka_ref, rk_ref, gg_ref, gb_ref, o_ref, s_ref, *, n_chunks):
    c = WKV_CHUNK

    @pl.when(pl.program_id(2) == 0)
    def _reset():
        s_ref[...] = jnp.zeros_like(s_ref)

    lane = lax.broadcasted_iota(jnp.int32, (c, LANES), 1)
    row = lax.broadcasted_iota(jnp.int32, (c, LANES), 0)
    col = jnp.bitwise_and(lane, HEAD_DIM - 1)
    h0 = lane < HEAD_DIM
    sq_r = lax.broadcasted_iota(jnp.int32, (c, c), 0)
    sq_c = lax.broadcasted_iota(jnp.int32, (c, c), 1)
    tri_incl = jnp.where(sq_c <= sq_r, 1.0, 0.0).astype(BF16)
    eye = jnp.where(sq_c == sq_r, 1.0, 0.0)
    st_r = lax.broadcasted_iota(jnp.int32, (LANES, LANES), 0)
    st_c = lax.broadcasted_iota(jnp.int32, (LANES, LANES), 1)
    same_head = (st_r < HEAD_DIM) == (st_c < HEAD_DIM)
    k_k, k_a, r_k, gn_g, gn_b = kk_ref[...], ka_ref[...], rk_ref[...], gg_ref[...], gb_ref[...]

    def head_sum(z):
        s0 = jnp.sum(jnp.where(h0, z, 0.0), axis=-1, keepdims=True)
        s1 = jnp.sum(jnp.where(h0, 0.0, z), axis=-1, keepdims=True)
        return jnp.where(h0, s0, s1)

    chunks = range(n_chunks)
    sl = [slice(ci * c, (ci + 1) * c) for ci in chunks]
    owns = (h0, jnp.logical_not(h0))
    keys = [(ci, h) for ci in chunks for h in range(2)]
    r_all, k_all, v_all, lw_all, a_all = r_ref[...], k_ref[...], v_ref[...], lw_ref[...], a_ref[...]
    lw_parts = _split3(lw_all)
    cum, kh, at, rt, bt_kt, bh_kh, e_tot, v, vv = [], [], [], [], [], [], [], [], []
    for ci in chunks:
        rs = sl[ci]
        cum.append(sum(_dot(tri_incl, part[rs]) for part in lw_parts))
    for ci in chunks:
        rs = sl[ci]
        k, a, lw = k_all[rs], a_all[rs], lw_all[rs]
        kk = k * k_k
        kk = kk / jnp.maximum(jnp.sqrt(head_sum(kk * kk)), 1e-12)
        kh.append(k * (1.0 + (a - 1.0) * k_a))
        bb = kk * a
        l_end = cum[ci][c - 1:c, :]
        e_end = jnp.exp(l_end - cum[ci])
        e_neg = jnp.exp(-cum[ci])
        e_tot.append(jnp.exp(l_end))
        at.append(-kk * jnp.exp(cum[ci] - lw))
        rt.append(r_all[rs] * jnp.exp(cum[ci]))
        bt_kt.append(jnp.concatenate([bb * e_neg, kh[ci] * e_neg], axis=0).astype(BF16))
        bh_kh.append(jnp.concatenate([bb * e_end, kh[ci] * e_end], axis=0).astype(BF16))
        v.append(v_all[rs])
        vv.append(jnp.concatenate([v[ci], v[ci]], axis=0).astype(BF16))
    at_own, a_top, a_bot, pw, inv = {}, {}, {}, {}, {}
    for key in keys:
        ci, h = key
        at_own[key] = jnp.where(owns[h], at[ci], 0.0)
        lhs = jnp.concatenate([at_own[key], jnp.where(owns[h], rt[ci], 0.0)], axis=0).astype(BF16)
        p = _dot_t(lhs, bt_kt[ci])
        a_top[key] = jnp.where(col < row, p[:c, :], 0.0)
        a_bot[key] = jnp.where(col <= row, p[c:, :], 0.0).astype(BF16)
        pw[key] = a_top[key][:, :c]
        inv[key] = eye + pw[key]
    span = 2
    while span < c:
        for key in keys:
            pw_b = pw[key].astype(BF16)
            pw[key] = _dot(pw_b, pw_b)
        for key in keys:
            inv[key] = inv[key] + _dot(pw[key].astype(BF16), inv[key].astype(BF16))
        span *= 2
    inv_b = {key: inv[key].astype(BF16) for key in keys}
    akv = {key: _dot(jnp.where(h0, 0.0, a_top[key]).astype(BF16), vv[key[0]]) for key in keys}
    at2h = {key: _dot(inv_b[key], at_own[key].astype(BF16)) for key in keys}
    uvh = {key: _dot(inv_b[key], akv[key].astype(BF16)) for key in keys}
    at2 = [(at2h[ci, 0] + at2h[ci, 1]).astype(BF16) for ci in chunks]
    uv = [jnp.where(h0, uvh[ci, 0], uvh[ci, 1]) for ci in chunks]
    w_step = [jnp.where(same_head, lax.dot_general(at2[ci], bh_kh[ci][:c], _CONTRACT_FIRST,
                                                   preferred_element_type=F32), 0.0).astype(BF16) for ci in chunks]
    n_step = [jnp.where(same_head, lax.dot_general(jnp.concatenate([uv[ci], v[ci]], axis=0).astype(BF16), bh_kh[ci],
                                                   _CONTRACT_FIRST, preferred_element_type=F32), 0.0) for ci in chunks]
    state = s_ref[...]
    entering = []
    for ci in chunks:
        state_b = state.astype(BF16)
        entering.append(state_b)
        state = state * e_tot[ci] + _dot(state_b, w_step[ci]) + n_step[ci]
    s_ref[...] = state
    u = [_dot_t(at2[ci], entering[ci]) + uv[ci] for ci in chunks]
    y_state = [_dot_t(rt[ci].astype(BF16), entering[ci]) for ci in chunks]
    for ci in chunks:
        u_v = jnp.concatenate([u[ci], v[ci]], axis=0).astype(BF16)
        y = y_state[ci] + jnp.where(h0, _dot(a_bot[ci, 0], u_v), _dot(a_bot[ci, 1], u_v))
        mu = head_sum(y) * (1.0 / HEAD_DIM)
        yc = y - mu
        var = head_sum(yc * yc) * (1.0 / HEAD_DIM)
        yn = yc * lax.rsqrt(var + GN_EPS) * gn_g + gn_b
        o_ref[sl[ci], :] = yn + head_sum(r_all[sl[ci]] * kh[ci] * r_k) * v[ci]


WKV_BLOCK = 1024


def _wkv(r, k, v, lw, a, k_k, k_a, r_k, gn_g, gn_b, batch, seq_len):
    t, d = r.shape
    assert seq_len % WKV_BLOCK == 0 and WKV_BLOCK % WKV_CHUNK == 0
    nt = seq_len // WKV_BLOCK
    tok = lambda bi, hp, ti: (bi * nt + ti, hp)
    par = lambda bi, hp, ti: (0, hp)
    vec = lambda z: z.reshape(1, d)
    return pl.pallas_call(
        functools.partial(_wkv_kernel, n_chunks=WKV_BLOCK // WKV_CHUNK),
        grid=(batch, N_PAIRS, nt),
        in_specs=[pl.BlockSpec((WKV_BLOCK, LANES), tok)] * 5 + [pl.BlockSpec((1, LANES), par)] * 5,
        out_specs=pl.BlockSpec((WKV_BLOCK, LANES), tok),
        out_shape=jax.ShapeDtypeStruct((t, d), F32),
        scratch_shapes=[pltpu.VMEM((LANES, LANES), F32)],
        compiler_params=_params("parallel", "parallel", "arbitrary"),
    )(r, k, v, lw, a, vec(k_k), vec(k_a), vec(r_k), vec(gn_g), vec(gn_b))


def _rwkv_layer(xt, b, s, v_first, mix, w_rkv, w0, w1, w2, a0, a1, a2, g1, g2, k_k, k_a, r_k, gn_g, gn_b, w_o, vres,
                ln_g, ln_b):
    r, k, v, lw, a, g = _rwkv_proj(xt, s, v_first, mix, w_rkv, w0, w1, w2, a0, a1, a2, g1, g2, vres)
    y = _wkv(r, k, v, lw, a, k_k, k_a, r_k, gn_g, gn_b, b, s)
    return _proj_ln(xt, y, w_o.astype(BF16), ln_g, ln_b, mult=g), (v if vres is None else v_first)


def _route_kernel(x_ref, w_ref, b_ref, pos_ref, gate_ref, off_ref, cnt_ref, *, tt):
    xh, xm, _ = _split3(x_ref[...])
    w = w_ref[...]
    wh = w.astype(BF16)
    wm = (w - wh.astype(F32)).astype(BF16)
    scores = jax.nn.sigmoid(_dot_t(wh, xh) + _dot_t(wh, xm) + _dot_t(wm, xh))
    biased = scores + b_ref[...]
    per = N_EXPERTS // N_GROUPS
    neg = -jnp.inf
    row_g = lax.broadcasted_iota(jnp.int32, (per, tt), 0)
    group_scores = []
    for g in range(N_GROUPS):
        grp = biased[g * per:(g + 1) * per, :]
        m1, i1 = _first_max(grp, row_g, per)
        m2 = jnp.max(jnp.where(row_g == i1, neg, grp), axis=0, keepdims=True)
        group_scores.append(m1 + m2)
    gsc = jnp.concatenate(group_scores, axis=0)
    row_n = lax.broadcasted_iota(jnp.int32, (N_GROUPS, tt), 0)
    keep = jnp.zeros((N_GROUPS, tt), F32)
    for _ in range(TOPK_GROUPS):
        _, ix = _first_max(gsc, row_n, N_GROUPS)
        pick = row_n == ix
        keep = jnp.where(pick, 1.0, keep)
        gsc = jnp.where(pick, neg, gsc)
    keep_e = jnp.concatenate([jnp.broadcast_to(keep[g:g + 1, :], (per, tt)) for g in range(N_GROUPS)], axis=0)
    cand = jnp.where(keep_e > 0.0, biased, neg)
    row_e = lax.broadcasted_iota(jnp.int32, (N_EXPERTS, tt), 0)
    picks = []
    for _ in range(TOP_K):
        _, ix = _first_max(cand, row_e, N_EXPERTS)
        pick = row_e == ix
        picks.append(pick)
        cand = jnp.where(pick, neg, cand)
    raw = [jnp.sum(jnp.where(p, scores, 0.0), axis=0, keepdims=True) for p in picks]
    total = raw[0]
    for r in raw[1:]:
        total = total + r
    gate_ref[0] = jnp.concatenate([r / total * ROUTED_SCALE for r in raw], axis=0)

    sel = jnp.zeros((N_EXPERTS, tt), F32)
    for p in picks:
        sel = jnp.where(p, 1.0, sel)
    cb = 256
    cr = lax.broadcasted_iota(jnp.int32, (cb, cb), 0)
    cc = lax.broadcasted_iota(jnp.int32, (cb, cb), 1)
    upper_incl = jnp.where(cr <= cc, 1.0, 0.0).astype(BF16)
    carry = jnp.zeros((N_EXPERTS, 1), F32)
    ranks = []
    for c0 in range(0, tt, cb):
        blk = sel[:, c0:c0 + cb]
        incl = _dot(blk.astype(BF16), upper_incl)
        ranks.append(incl - blk + carry)
        carry = carry + incl[:, cb - 1:cb]
    rank = jnp.concatenate(ranks, axis=1)
    count = jnp.broadcast_to(carry, (N_EXPERTS, LANES))
    aligned8 = jnp.floor((count + 7.0) * 0.125)
    er = lax.broadcasted_iota(jnp.int32, (N_EXPERTS, N_EXPERTS), 0)
    ec = lax.broadcasted_iota(jnp.int32, (N_EXPERTS, N_EXPERTS), 1)
    strict_lower = jnp.where(ec < er, 1.0, 0.0).astype(BF16)
    start = 8.0 * _dot(strict_lower, aligned8.astype(BF16))
    where_to = (start[:, :1] + rank) * float(D_MODEL // LANES)
    pos_ref[0] = jnp.concatenate(
        [jnp.sum(jnp.where(p, where_to, 0.0), axis=0, keepdims=True) for p in picks], axis=0).astype(jnp.int32)
    off_ref[0] = start.astype(jnp.int32)
    cnt_ref[0] = count.astype(jnp.int32)


MOE_TILE = 1024
MOE_CHUNK = 160
EXPERTS_PER_STEP = 2
SEG_ALIGN = 8
ROW_LOOP_UNROLL = 4


def _route_tiles(xt, w_router, router_bias):
    t, d = xt.shape
    tt = MOE_TILE
    n_tiles = t // tt
    tile3 = lambda i: (i, 0, 0)
    pos, gate, off, cnt = pl.pallas_call(
        functools.partial(_route_kernel, tt=tt),
        grid=(n_tiles,),
        in_specs=[pl.BlockSpec((tt, d), lambda i: (i, 0)),
                  pl.BlockSpec((N_EXPERTS, d), lambda i: (0, 0)),
                  pl.BlockSpec((N_EXPERTS, 1), lambda i: (0, 0))],
        out_specs=[pl.BlockSpec((1, TOP_K, tt), tile3), pl.BlockSpec((1, TOP_K, tt), tile3),
                   pl.BlockSpec((1, N_EXPERTS, LANES), tile3), pl.BlockSpec((1, N_EXPERTS, LANES), tile3)],
        out_shape=[jax.ShapeDtypeStruct((n_tiles, TOP_K, tt), jnp.int32),
                   jax.ShapeDtypeStruct((n_tiles, TOP_K, tt), F32),
                   jax.ShapeDtypeStruct((n_tiles, N_EXPERTS, LANES), jnp.int32),
                   jax.ShapeDtypeStruct((n_tiles, N_EXPERTS, LANES), jnp.int32)],
        compiler_params=_params("parallel"),
    )(xt, w_router.astype(F32).T, router_bias.astype(F32).reshape(N_EXPERTS, 1))
    token_major = lambda z: z.transpose(0, 2, 1).reshape(n_tiles, tt * TOP_K)
    return token_major(pos), token_major(gate), off[:, :, 0].reshape(-1), cnt[:, :, 0].reshape(-1)


def _moe_tile_kernel(off_ref, cnt_ref, nch_ref, x_hbm, pos_hbm, gate_hbm, wg_ref, wu_ref, wd_ref, o_hbm,
                     stage, tok, pos_s, gate_s, sems, *, tt):
    ti = pl.program_id(0)
    step = pl.program_id(1)
    sub = D_MODEL // LANES
    rows = MOE_CHUNK

    def tile_copy(src, dst, sem):
        return pltpu.make_async_copy(src, dst, sem)

    @pl.when(step == 0)
    def _distribute():
        copies = [tile_copy(x_hbm.at[pl.ds(pl.multiple_of(ti * tt * sub, tt * sub), tt * sub)], tok, sems.at[0]),
                  tile_copy(pos_hbm.at[ti], pos_s, sems.at[1]),
                  tile_copy(gate_hbm.at[ti], gate_s, sems.at[2])]
        for c in copies:
            c.start()
        for c in copies:
            c.wait()
        zero_group = jnp.zeros((SEG_ALIGN * sub, LANES), F32)

        def clear_tail(ex, carry):
            end = off_ref[ti * N_EXPERTS + ex] + cnt_ref[ti * N_EXPERTS + ex]
            g0 = pl.multiple_of(lax.shift_left(lax.shift_right_logical(end, 3), 3) * sub, SEG_ALIGN * sub)
            stage[pl.ds(g0, SEG_ALIGN * sub), :] = zero_group
            return carry

        lax.fori_loop(0, N_EXPERTS, clear_tail, 0)
        last = ti * N_EXPERTS + N_EXPERTS - 1
        total = off_ref[last] + lax.shift_left(lax.shift_right_logical(cnt_ref[last] + 7, 3), 3)
        t0 = pl.multiple_of(total * sub, SEG_ALIGN * sub)
        stage[pl.ds(t0, rows * sub), :] = jnp.zeros((rows * sub, LANES), F32)

        def place(tq, carry):
            for t in [ROW_LOOP_UNROLL * tq + u for u in range(ROW_LOOP_UNROLL)]:
                row = tok[pl.ds(pl.multiple_of(t * sub, sub), sub), :]
                for j in range(TOP_K):
                    stage[pl.ds(pl.multiple_of(pos_s[t * TOP_K + j], sub), sub), :] = row
            return carry

        lax.fori_loop(0, tt // ROW_LOOP_UNROLL, place, 0)

    row_id = lax.broadcasted_iota(jnp.int32, (rows, LANES), 0)
    experts = []
    for i in range(EXPERTS_PER_STEP):
        e = (ti * (N_EXPERTS // EXPERTS_PER_STEP) + step) * EXPERTS_PER_STEP + i
        experts.append((i, off_ref[e], cnt_ref[e], nch_ref[e]))
    n_steps = experts[0][3]
    for ex in experts[1:]:
        n_steps = jnp.maximum(n_steps, ex[3])

    def ffn_step(c, carry):
        loaded = []
        for i, off, cnt, n_mine in experts:
            c_mine = jnp.minimum(c, jnp.maximum(n_mine - 1, 0))
            base = pl.multiple_of((off + c_mine * rows) * sub, SEG_ALIGN * sub)
            loaded.append((base, [stage[pl.ds(base + k, rows, stride=sub), :] for k in range(sub)]))
        outs = []
        for (i, off, cnt, n_mine), (base, parts) in zip(experts, loaded):
            xb = jnp.concatenate(parts, axis=1).astype(BF16)
            h = jax.nn.silu(_dot(xb, wg_ref[i])) * _dot(xb, wu_ref[i])
            outs.append(_dot(h.astype(BF16), wd_ref[i]))
        for (i, off, cnt, n_mine), (base, parts), y in zip(experts, loaded, outs):
            live = row_id < cnt - c * rows
            for k in range(sub):
                stage[pl.ds(base + k, rows, stride=sub), :] = jnp.where(live, y[:, k * LANES:(k + 1) * LANES], parts[k])
        return carry

    lax.fori_loop(0, n_steps, ffn_step, 0)

    @pl.when(step == N_EXPERTS // EXPERTS_PER_STEP - 1)
    def _combine():
        def gather(tq, carry):
            for t in [ROW_LOOP_UNROLL * tq + u for u in range(ROW_LOOP_UNROLL)]:
                acc = jnp.zeros((sub, LANES), F32)
                for j in range(TOP_K):
                    p = pl.multiple_of(pos_s[t * TOP_K + j], sub)
                    acc = acc + gate_s[t * TOP_K + j] * stage[pl.ds(p, sub), :]
                tok[pl.ds(pl.multiple_of(t * sub, sub), sub), :] = acc
            return carry

        lax.fori_loop(0, tt // ROW_LOOP_UNROLL, gather, 0)
        out = tile_copy(tok, o_hbm.at[pl.ds(pl.multiple_of(ti * tt * sub, tt * sub), tt * sub)], sems.at[0])
        out.start()
        out.wait()


def _routed_experts(x8, pos, gate, off, cnt, w_gate, w_up, w_down):
    sub = SUBROWS
    t, d = x8.shape[0] // sub, D_MODEL
    tt = MOE_TILE
    n_tiles = t // tt
    stage_rows = TOP_K * tt + N_EXPERTS * SEG_ALIGN + MOE_CHUNK
    any_spec = pl.BlockSpec(memory_space=pl.ANY)
    grid_spec = pltpu.PrefetchScalarGridSpec(
        num_scalar_prefetch=3,
        grid=(n_tiles, N_EXPERTS // EXPERTS_PER_STEP),
        in_specs=[any_spec, any_spec, any_spec,
                  pl.BlockSpec((EXPERTS_PER_STEP, d, D_EXPERT), lambda ti, e, *_: (e, 0, 0)),
                  pl.BlockSpec((EXPERTS_PER_STEP, d, D_EXPERT), lambda ti, e, *_: (e, 0, 0)),
                  pl.BlockSpec((EXPERTS_PER_STEP, D_EXPERT, d), lambda ti, e, *_: (e, 0, 0))],
        out_specs=any_spec,
        scratch_shapes=[pltpu.VMEM((stage_rows * sub, LANES), F32),
                        pltpu.VMEM((tt * sub, LANES), F32),
                        pltpu.SMEM((TOP_K * tt,), jnp.int32),
                        pltpu.SMEM((TOP_K * tt,), F32),
                        pltpu.SemaphoreType.DMA((3,))],
    )
    return pl.pallas_call(
        functools.partial(_moe_tile_kernel, tt=tt),
        grid_spec=grid_spec,
        out_shape=jax.ShapeDtypeStruct((t * sub, LANES), F32),
        compiler_params=_params("arbitrary", "arbitrary"),
    )(off, cnt, (cnt + MOE_CHUNK - 1) // MOE_CHUNK, x8, pos, gate, w_gate, w_up, w_down)


def _shared_ln_kernel(x_ref, r8_ref, sg_ref, su_ref, sd_ref, g_ref, b_ref, o_ref):
    x = x_ref[...]
    xb = x.astype(BF16)
    h = jax.nn.silu(_dot(xb, sg_ref[...])) * _dot(xb, su_ref[...])
    f = _load_row_major(r8_ref, x.shape[0]) + _dot(h.astype(BF16), sd_ref[...])
    o_ref[...] = _layer_norm_rows(ALPHA * x + f, g_ref[...], b_ref[...])


def _shared_ln(xt, routed8, sh_gate, sh_up, sh_down, g, b):
    t, d = xt.shape
    row = lambda i: (i, 0)
    fix = lambda i: (0, 0)
    return pl.pallas_call(
        _shared_ln_kernel,
        grid=(t // ROW_BLOCK,),
        in_specs=[pl.BlockSpec((ROW_BLOCK, d), row), pl.BlockSpec((ROW_BLOCK * SUBROWS, LANES), row),
                  pl.BlockSpec((d, D_EXPERT), fix), pl.BlockSpec((d, D_EXPERT), fix), pl.BlockSpec((D_EXPERT, d), fix),
                  pl.BlockSpec((1, d), fix), pl.BlockSpec((1, d), fix)],
        out_specs=pl.BlockSpec((ROW_BLOCK, d), row),
        out_shape=jax.ShapeDtypeStruct((t, d), F32),
        compiler_params=_params("parallel"),
    )(xt, routed8, sh_gate.astype(BF16), sh_up.astype(BF16), sh_down.astype(BF16), g.reshape(1, d), b.reshape(1, d))


def _moe_layer(xt, x8, w_router, router_bias, w_gate, w_up, w_down, sh_gate, sh_up, sh_down, ln_g, ln_b):
    assert xt.shape[0] % MOE_TILE == 0
    pos, gate, off, cnt = _route_tiles(xt, w_router, router_bias)
    routed8 = _routed_experts(x8, pos, gate, off, cnt, w_gate.astype(BF16), w_up.astype(BF16), w_down.astype(BF16))
    return _shared_ln(xt, routed8, sh_gate, sh_up, sh_down, ln_g, ln_b)


def kernel(x, moba_w_qkv, moba_w_o, rel_bias, rwkv_mix, rwkv_w_rkv, rwkv_w0, rwkv_w1, rwkv_w2,
           rwkv_a0, rwkv_a1, rwkv_a2, rwkv_v0, rwkv_v1, rwkv_v2, rwkv_g1, rwkv_g2, rwkv_k_k,
           rwkv_k_a, rwkv_r_k, rwkv_gn_g, rwkv_gn_b, rwkv_w_o, moe_w_router, moe_router_bias,
           moe_w_gate, moe_w_up, moe_w_down, moe_sh_gate, moe_sh_up, moe_sh_down,
           ln_mix_g, ln_mix_b, ln_ffn_g, ln_ffn_b):
    assert x.shape[-1] == D_MODEL
    own_bias, prev_bias = _moba_bias_tables(rel_bias)
    b, s, d = x.shape
    xt = x.reshape(b * s, d)
    v_first = None
    for i in range(DEPTH):
        j = i // 2
        if i % 2 == 0:
            xt, x8 = _moba_layer(xt, b, s, moba_w_qkv[j], moba_w_o[j], own_bias, prev_bias, ln_mix_g[i], ln_mix_b[i])
        else:
            vres = None if j == 0 else (rwkv_v0[j - 1], rwkv_v1[j - 1], rwkv_v2[j - 1])
            (xt, x8), v_first = _rwkv_layer(xt, b, s, v_first, rwkv_mix[j], rwkv_w_rkv[j], rwkv_w0[j], rwkv_w1[j],
                                            rwkv_w2[j], rwkv_a0[j], rwkv_a1[j], rwkv_a2[j], rwkv_g1[j],
                                            rwkv_g2[j], rwkv_k_k[j], rwkv_k_a[j], rwkv_r_k[j],
                                            rwkv_gn_g[j], rwkv_gn_b[j], rwkv_w_o[j], vres, ln_mix_g[i], ln_mix_b[i])
        xt = _moe_layer(xt, x8, moe_w_router[i], moe_router_bias[i], moe_w_gate[i], moe_w_up[i], moe_w_down[i],
                        moe_sh_gate[i], moe_sh_up[i], moe_sh_down[i], ln_ffn_g[i], ln_ffn_b[i])
    return xt.reshape(b, s, d)
```

```python
import functools
import math

import jax
import jax.numpy as jnp
import numpy as np
from jax import lax
from jax.experimental import pallas as pl
from jax.experimental.pallas import tpu as pltpu

F32 = jnp.float32
BF16 = jnp.bfloat16

D_MODEL = 1024
DEPTH = 4
ALPHA = (2 * DEPTH) ** 0.25
LN_EPS = 1e-5

HEAD_DIM = 64
LANES = 128
N_PAIRS = D_MODEL // LANES
MOBA_BLOCK = 256
MOBA_TOPK = 3
N_BUCKETS = 32
MAX_DISTANCE = 128
MASKED = -1e30
LOG2_E = math.log2(math.e)

GN_EPS = 64e-5
WKV_CHUNK = 64

N_EXPERTS = 64
N_GROUPS = 8
TOPK_GROUPS = 4
TOP_K = 8
D_EXPERT = 256
ROUTED_SCALE = 2.5

VMEM_LIMIT = 56 * 1024 * 1024
ROW_BLOCK = 512

_CONTRACT_LAST = (((1,), (1,)), ((), ()))
_CONTRACT_FIRST = (((0,), (0,)), ((), ()))


def _params(*sem):
    return pltpu.CompilerParams(dimension_semantics=sem, vmem_limit_bytes=VMEM_LIMIT)


def _dot(a, b):
    return jnp.dot(a, b, preferred_element_type=F32)


def _dot_t(a, b):
    return lax.dot_general(a, b, _CONTRACT_LAST, preferred_element_type=F32)


def _split3(x):
    hi = x.astype(BF16)
    r1 = x - hi.astype(F32)
    mid = r1.astype(BF16)
    lo = (r1 - mid.astype(F32)).astype(BF16)
    return hi, mid, lo


def _first_max(vals, idx, big):
    mx = jnp.max(vals, axis=0, keepdims=True)
    return mx, jnp.min(jnp.where(vals == mx, idx, big), axis=0, keepdims=True)


def _layer_norm_rows(y, g, b):
    mu = jnp.mean(y, axis=-1, keepdims=True)
    yc = y - mu
    var = jnp.mean(yc * yc, axis=-1, keepdims=True)
    return yc * lax.rsqrt(var + LN_EPS) * g + b


def _mm_kernel(x_ref, w_ref, o_ref):
    x = x_ref[...].astype(BF16)
    o_ref[...] = _dot(x, w_ref[...]).astype(o_ref.dtype)


def _matmul(x, w, out_dtype):
    m, k = x.shape
    n = w.shape[1]
    return pl.pallas_call(
        _mm_kernel,
        grid=(m // ROW_BLOCK,),
        in_specs=[pl.BlockSpec((ROW_BLOCK, k), lambda i: (i, 0)),
                  pl.BlockSpec((k, n), lambda i: (0, 0))],
        out_specs=pl.BlockSpec((ROW_BLOCK, n), lambda i: (i, 0)),
        out_shape=jax.ShapeDtypeStruct((m, n), out_dtype),
        compiler_params=_params("parallel"),
    )(x, w)


SUBROWS = D_MODEL // LANES


def _store_row_major(o8_ref, y):
    for k in range(SUBROWS):
        o8_ref[pl.ds(k, y.shape[0], stride=SUBROWS), :] = y[:, k * LANES:(k + 1) * LANES]


def _load_row_major(r8_ref, rows):
    return jnp.concatenate([r8_ref[pl.ds(k, rows, stride=SUBROWS), :] for k in range(SUBROWS)], axis=1)


def _proj_ln_kernel(x_ref, a_ref, w_ref, g_ref, b_ref, o_ref, o8_ref):
    h = _dot(a_ref[...].astype(BF16), w_ref[...])
    y = _layer_norm_rows(ALPHA * x_ref[...] + h, g_ref[...], b_ref[...])
    o_ref[...] = y
    _store_row_major(o8_ref, y)


def _proj_gate_ln_kernel(x_ref, a_ref, m_ref, w_ref, g_ref, b_ref, o_ref, o8_ref):
    h = _dot((a_ref[...] * m_ref[...]).astype(BF16), w_ref[...])
    y = _layer_norm_rows(ALPHA * x_ref[...] + h, g_ref[...], b_ref[...])
    o_ref[...] = y
    _store_row_major(o8_ref, y)


def _proj_ln(x, a, w, g, b, mult=None):
    t, d = x.shape
    k = a.shape[1]
    row = lambda i: (i, 0)
    fix = lambda i: (0, 0)
    acts = [a] if mult is None else [a, mult]
    kern = _proj_ln_kernel if mult is None else _proj_gate_ln_kernel
    return pl.pallas_call(
        kern,
        grid=(t // ROW_BLOCK,),
        in_specs=[pl.BlockSpec((ROW_BLOCK, d), row)]
        + [pl.BlockSpec((ROW_BLOCK, k), row) for _ in acts]
        + [pl.BlockSpec((k, d), fix), pl.BlockSpec((1, d), fix), pl.BlockSpec((1, d), fix)],
        out_specs=[pl.BlockSpec((ROW_BLOCK, d), row), pl.BlockSpec((ROW_BLOCK * SUBROWS, LANES), row)],
        out_shape=[jax.ShapeDtypeStruct((t, d), F32), jax.ShapeDtypeStruct((t * SUBROWS, LANES), F32)],
        compiler_params=_params("parallel"),
    )(x, *acts, w, g.reshape(1, d), b.reshape(1, d))


def _rel_bucket_np(dist):
    n = np.maximum(dist, 0)
    max_exact = N_BUCKETS // 2
    nf = np.maximum(n, 1).astype(np.float32)
    large = max_exact + (np.log(nf / np.float32(max_exact)) / np.float32(math.log(MAX_DISTANCE / max_exact))
                         * np.float32(N_BUCKETS - max_exact)).astype(np.int32)
    return np.where(n < max_exact, n, np.minimum(large, N_BUCKETS - 1))


def _moba_bias_tables(rel_bias):
    off = np.arange(MOBA_BLOCK)
    rel = off[:, None] - off[None, :]
    assert int(_rel_bucket_np(np.array([MOBA_BLOCK + 1]))[0]) == N_BUCKETS - 1
    shifted = (rel_bias - rel_bias[N_BUCKETS - 1:]).astype(F32) * LOG2_E

    def lookup(buckets):
        onehot = (jnp.asarray(buckets.reshape(-1, 1)) == jnp.arange(N_BUCKETS)[None, :]).astype(F32)
        table = jnp.dot(onehot, shifted, precision=lax.Precision.HIGHEST)
        return table.T.reshape(-1, MOBA_BLOCK, MOBA_BLOCK)

    own = jnp.where(jnp.asarray(rel >= 0)[None], lookup(_rel_bucket_np(rel)), MASKED)
    return own, lookup(_rel_bucket_np(rel + MOBA_BLOCK))


def _grouped_loop(n, body):
    def eight(i, carry):
        for u in range(8):
            body(8 * i + u)
        return carry

    lax.fori_loop(0, lax.shift_right_logical(n, 3), eight, 0)

    @pl.when(jnp.bitwise_and(n, 4) == 4)
    def _four_more():
        done = jnp.bitwise_and(n, -8)
        for u in range(4):
            body(done + u)

    @pl.when(jnp.bitwise_and(n, 2) == 2)
    def _two_more():
        done = jnp.bitwise_and(n, -4)
        body(done)
        body(done + 1)

    @pl.when(jnp.bitwise_and(n, 1) == 1)
    def _last():
        body(n - 1)


def _moba_gate_kernel(q_ref, k_ref, g_ref, *, nblk):
    blk = MOBA_BLOCK
    nbp = -(-nblk // 8) * 8
    s = nblk * blk
    rows = [jnp.sum(k_ref[0, j * blk:(j + 1) * blk, :].astype(F32), axis=0, keepdims=True) * (1.0 / blk)
            for j in range(nblk)]
    if nbp > nblk:
        rows.append(jnp.zeros((nbp - nblk, LANES), F32))
    km = jnp.concatenate(rows, axis=0)
    lane_k = lax.broadcasted_iota(jnp.int32, (nbp, LANES), 1)
    q = q_ref[0]
    blk_id = lax.broadcasted_iota(jnp.int32, (nbp, s), 0)
    q_blk = lax.shift_right_logical(lax.broadcasted_iota(jnp.int32, (nbp, s), 1), blk.bit_length() - 1)
    valid = blk_id < q_blk
    src_blk = lax.broadcasted_iota(jnp.int32, (nbp, LANES), 0)
    total = None
    for h in range(2):
        kmf = jnp.where((lane_k < HEAD_DIM) if h == 0 else (lane_k >= HEAD_DIM), km, 0.0)
        kmh = kmf.astype(BF16)
        kml = (kmf - kmh.astype(F32)).astype(BF16)
        gm = jnp.where(valid, _dot_t(kmh, q) + _dot_t(kml, q), -jnp.inf)
        hidden = jnp.where(valid, 1.0, 0.0)
        for _ in range(MOBA_TOPK):
            mx, ix = _first_max(gm, blk_id, nbp)
            pick = (blk_id == ix) & (mx > -jnp.inf)
            hidden = jnp.where(pick, 0.0, hidden)
            gm = jnp.where(pick, -jnp.inf, gm)
        goff = HEAD_DIM if h == 0 else 0
        place = jnp.where(lane_k == src_blk + goff, MASKED, 0.0).astype(BF16)
        gsel = lax.dot_general(hidden.astype(BF16), place, _CONTRACT_FIRST, preferred_element_type=F32)
        total = gsel if total is None else total + gsel
    g_ref[0] = total.astype(g_ref.dtype)


def _moba_gates(qkv, nblk):
    b, s, _ = qkv.shape
    return pl.pallas_call(
        functools.partial(_moba_gate_kernel, nblk=nblk),
        grid=(b, N_PAIRS),
        in_specs=[pl.BlockSpec((1, s, LANES), lambda bi, hp: (bi, 0, hp)),
                  pl.BlockSpec((1, s, LANES), lambda bi, hp: (bi, 0, N_PAIRS + hp))],
        out_specs=pl.BlockSpec((1, s, LANES), lambda bi, hp: (bi, 0, hp)),
        out_shape=jax.ShapeDtypeStruct((b, s, D_MODEL), BF16),
        compiler_params=_params("parallel", "parallel"),
    )(qkv, qkv)


def _moba_kernel(q_ref, g_ref, k_ref, v_ref, own_ref, prev_ref, o_ref, s_ref, mx_ref, acc_ref):
    qb = pl.program_id(2)
    blk = MOBA_BLOCK
    lane = lax.broadcasted_iota(jnp.int32, (blk, LANES), 1)
    q = q_ref[0]
    g = g_ref[0]
    heads = []
    for h in range(2):
        own = (lane < HEAD_DIM) if h == 0 else (lane >= HEAD_DIM)
        goff = HEAD_DIM if h == 0 else 0
        q_aug = jnp.where(own, q, g)
        q_own = jnp.where(own, q, jnp.zeros_like(q))
        heads.append((own, goff, q_aug, q_own))

    def block_rows(j):
        return pl.ds(pl.multiple_of(j * blk, blk), blk)

    kj = k_ref[0, block_rows(qb), :]
    for h, (own, goff, q_aug, q_own) in enumerate(heads):
        s = _dot_t(q_own, kj) + own_ref[h]
        s_ref[h, qb] = s
        mx_ref[h] = s

    def past_logits(j, with_prev_bias):
        kj = k_ref[0, block_rows(j), :]
        for h, (own, goff, q_aug, q_own) in enumerate(heads):
            ej = jnp.where(lane == goff + j, 1.0, 0.0).astype(BF16)
            s = _dot_t(q_aug, jnp.where(own, kj, ej))
            if with_prev_bias:
                s = s + prev_ref[h]
            s_ref[h, j] = s
            mx_ref[h] = jnp.maximum(mx_ref[h], s)

    @pl.when(qb >= 1)
    def _prev_block():
        past_logits(qb - 1, True)

    _grouped_loop(jnp.maximum(qb - 1, 0), lambda j: past_logits(j, False))

    for h in range(2):
        mm = mx_ref[h]
        row_max = jnp.max(jnp.maximum(mm[:, :LANES], mm[:, LANES:]), axis=-1, keepdims=True)
        mx_ref[h] = jnp.broadcast_to(row_max, (blk, blk))
    acc_ref[...] = jnp.zeros_like(acc_ref)

    def accumulate(j):
        vj = v_ref[0, block_rows(j), :]
        for h, (own, goff, q_aug, q_own) in enumerate(heads):
            p = jnp.exp2(s_ref[h, j] - mx_ref[h])
            acc_ref[h] = acc_ref[h] + _dot(p.astype(BF16), jnp.where(own, vj, jnp.ones_like(vj)))

    _grouped_loop(qb + 1, accumulate)

    a0 = acc_ref[0]
    a1 = acc_ref[1]
    o0 = a0 / pltpu.roll(a0, HEAD_DIM, axis=1)
    o1 = a1 / pltpu.roll(a1, HEAD_DIM, axis=1)
    o_ref[0] = jnp.where(lane < HEAD_DIM, o0, o1).astype(o_ref.dtype)


def _moba_attention(qkv, own_bias, prev_bias):
    b, s, _ = qkv.shape
    assert s % MOBA_BLOCK == 0
    nblk = s // MOBA_BLOCK
    assert nblk <= HEAD_DIM
    return pl.pallas_call(
        _moba_kernel,
        grid=(b, N_PAIRS, nblk),
        in_specs=[
            pl.BlockSpec((1, MOBA_BLOCK, LANES), lambda bi, hp, qb: (bi, qb, hp)),
            pl.BlockSpec((1, MOBA_BLOCK, LANES), lambda bi, hp, qb: (bi, qb, hp)),
            pl.BlockSpec((1, s, LANES), lambda bi, hp, qb: (bi, 0, N_PAIRS + hp)),
            pl.BlockSpec((1, s, LANES), lambda bi, hp, qb: (bi, 0, 2 * N_PAIRS + hp)),
            pl.BlockSpec((2, MOBA_BLOCK, MOBA_BLOCK), lambda bi, hp, qb: (hp, 0, 0)),
            pl.BlockSpec((2, MOBA_BLOCK, MOBA_BLOCK), lambda bi, hp, qb: (hp, 0, 0)),
        ],
        out_specs=pl.BlockSpec((1, MOBA_BLOCK, LANES), lambda bi, hp, qb: (bi, qb, hp)),
        out_shape=jax.ShapeDtypeStruct((b, s, D_MODEL), BF16),
        scratch_shapes=[pltpu.VMEM((2, nblk, MOBA_BLOCK, MOBA_BLOCK), F32),
                        pltpu.VMEM((2, MOBA_BLOCK, MOBA_BLOCK), F32),
                        pltpu.VMEM((2, MOBA_BLOCK, LANES), F32)],
        compiler_params=_params("parallel", "parallel", "arbitrary"),
    )(qkv, _moba_gates(qkv, nblk), qkv, qkv, own_bias, prev_bias)


def _moba_layer(xt, b, s, w_qkv, w_o, own_bias, prev_bias, ln_g, ln_b):
    d = xt.shape[1]
    scale = jnp.concatenate([jnp.full((d,), HEAD_DIM ** -0.5 * LOG2_E, F32), jnp.ones((2 * d,), F32)])
    qkv = _matmul(xt, (w_qkv * scale).astype(BF16), BF16)
    o = _moba_attention(qkv.reshape(b, s, 3 * d), own_bias, prev_bias)
    return _proj_ln(xt, o.reshape(b * s, d), w_o.astype(BF16), ln_g, ln_b)


def _softplus(u):
    return jnp.maximum(u, 0.0) + jnp.log1p(jnp.exp(-jnp.abs(u)))


def _rwkv_proj_kernel(*refs, seq_blocks, has_vres):
    if has_vres:
        (x_ref, xp_ref, vf_ref, mix_ref, wr_ref, wk_ref, wv_ref, w0_ref, w1_ref, w2_ref, a0_ref, a1_ref, a2_ref,
         g1_ref, g2_ref, v0_ref, v1_ref, v2_ref, r_ref, k_ref, v_ref, lw_ref, a_ref, g_ref) = refs
    else:
        (x_ref, xp_ref, mix_ref, wr_ref, wk_ref, wv_ref, w0_ref, w1_ref, w2_ref, a0_ref, a1_ref, a2_ref,
         g1_ref, g2_ref, r_ref, k_ref, v_ref, lw_ref, a_ref, g_ref) = refs
    x = x_ref[...]
    rows = lax.broadcasted_iota(jnp.int32, x.shape, 0)
    first_in_seq = pl.program_id(0) % seq_blocks == 0
    carry_row = jnp.where(first_in_seq, 0.0, xp_ref[7:8, :])
    x_prev = jnp.where(rows == 0, carry_row, pltpu.roll(x, 1, axis=0))
    xx = x_prev - x

    def mixed(i):
        return (x + xx * mix_ref[i:i + 1, :]).astype(BF16)

    def lora(inp, w_in, w_out, act=None):
        mid = _dot(inp, w_in[...])
        if act is not None:
            mid = act(mid)
        return _dot(mid.astype(BF16), w_out[...])

    r_ref[...] = _dot(mixed(0), wr_ref[...])
    k_ref[...] = _dot(mixed(1), wk_ref[...])
    xv = mixed(2)
    v = _dot(xv, wv_ref[...])
    if has_vres:
        v = v + (vf_ref[...] - v) * jax.nn.sigmoid(v0_ref[...] + lora(xv, v1_ref, v2_ref))
    v_ref[...] = v
    w_log = -_softplus(-(w0_ref[...] + lora(mixed(3), w1_ref, w2_ref, jnp.tanh))) - 0.5
    lw_ref[...] = -jnp.exp(w_log)
    a_ref[...] = jax.nn.sigmoid(a0_ref[...] + lora(mixed(4), a1_ref, a2_ref))
    g_ref[...] = lora(mixed(5), g1_ref, g2_ref, jax.nn.sigmoid)


def _rwkv_proj(xt, seq_len, v_first, mix, w_rkv, w0, w1, w2, a0, a1, a2, g1, g2, vres):
    t, d = xt.shape
    assert seq_len % ROW_BLOCK == 0
    row = lambda i: (i, 0)
    fix = lambda i: (0, 0)
    prev8 = lambda i: (jnp.maximum(i * (ROW_BLOCK // 8) - 1, 0), 0)
    has_vres = vres is not None
    vec = lambda z: z.reshape(1, d)
    bf = lambda z: z.astype(BF16)
    ins = [xt, xt] + ([v_first] if has_vres else []) + [
        mix, bf(w_rkv[0]), bf(w_rkv[1]), bf(w_rkv[2]), vec(w0), bf(w1), bf(w2), vec(a0), bf(a1), bf(a2), bf(g1), bf(g2)]
    if has_vres:
        ins += [vec(vres[0]), bf(vres[1]), bf(vres[2])]
    specs = [pl.BlockSpec((ROW_BLOCK, d), row), pl.BlockSpec((8, d), prev8)]
    if has_vres:
        specs.append(pl.BlockSpec((ROW_BLOCK, d), row))
    specs += [pl.BlockSpec(z.shape, fix) for z in ins[len(specs):]]
    return pl.pallas_call(
        functools.partial(_rwkv_proj_kernel, seq_blocks=seq_len // ROW_BLOCK, has_vres=has_vres),
        grid=(t // ROW_BLOCK,),
        in_specs=specs,
        out_specs=[pl.BlockSpec((ROW_BLOCK, d), row)] * 6,
        out_shape=[jax.ShapeDtypeStruct((t, d), F32)] * 6,
        compiler_params=_params("parallel"),
    )(*ins)


def _wkv_kernel(r_ref, k_ref, v_ref, lw_ref, a_ref, kk_ref, ka_ref, rk_ref, gg_ref, gb_ref, o_ref, s_ref, *, n_chunks):
    c = WKV_CHUNK

    @pl.when(pl.program_id(2) == 0)
    def _reset():
        s_ref[...] = jnp.zeros_like(s_ref)

    lane = lax.broadcasted_iota(jnp.int32, (c, LANES), 1)
    row = lax.broadcasted_iota(jnp.int32, (c, LANES), 0)
    col = jnp.bitwise_and(lane, HEAD_DIM - 1)
    h0 = lane < HEAD_DIM
    sq_r = lax.broadcasted_iota(jnp.int32, (c, c), 0)
    sq_c = lax.broadcasted_iota(jnp.int32, (c, c), 1)
    tri_incl = jnp.where(sq_c <= sq_r, 1.0, 0.0).astype(BF16)
    eye = jnp.where(sq_c == sq_r, 1.0, 0.0)
    st_r = lax.broadcasted_iota(jnp.int32, (LANES, LANES), 0)
    st_c = lax.broadcasted_iota(jnp.int32, (LANES, LANES), 1)
    same_head = (st_r < HEAD_DIM) == (st_c < HEAD_DIM)
    k_k, k_a, r_k, gn_g, gn_b = kk_ref[...], ka_ref[...], rk_ref[...], gg_ref[...], gb_ref[...]

    def head_sum(z):
        s0 = jnp.sum(jnp.where(h0, z, 0.0), axis=-1, keepdims=True)
        s1 = jnp.sum(jnp.where(h0, 0.0, z), axis=-1, keepdims=True)
        return jnp.where(h0, s0, s1)

    chunks = range(n_chunks)
    sl = [slice(ci * c, (ci + 1) * c) for ci in chunks]
    owns = (h0, jnp.logical_not(h0))
    keys = [(ci, h) for ci in chunks for h in range(2)]
    r_all, k_all, v_all, lw_all, a_all = r_ref[...], k_ref[...], v_ref[...], lw_ref[...], a_ref[...]
    lw_parts = _split3(lw_all)
    cum, kh, at, rt, bt_kt, bh_kh, e_tot, v, vv = [], [], [], [], [], [], [], [], []
    for ci in chunks:
        rs = sl[ci]
        cum.append(sum(_dot(tri_incl, part[rs]) for part in lw_parts))
    for ci in chunks:
        rs = sl[ci]
        k, a, lw = k_all[rs], a_all[rs], lw_all[rs]
        kk = k * k_k
        kk = kk / jnp.maximum(jnp.sqrt(head_sum(kk * kk)), 1e-12)
        kh.append(k * (1.0 + (a - 1.0) * k_a))
        bb = kk * a
        l_end = cum[ci][c - 1:c, :]
        e_end = jnp.exp(l_end - cum[ci])
        e_neg = jnp.exp(-cum[ci])
        e_tot.append(jnp.exp(l_end))
        at.append(-kk * jnp.exp(cum[ci] - lw))
        rt.append(r_all[rs] * jnp.exp(cum[ci]))
        bt_kt.append(jnp.concatenate([bb * e_neg, kh[ci] * e_neg], axis=0).astype(BF16))
        bh_kh.append(jnp.concatenate([bb * e_end, kh[ci] * e_end], axis=0).astype(BF16))
        v.append(v_all[rs])
        vv.append(jnp.concatenate([v[ci], v[ci]], axis=0).astype(BF16))
    at_own, a_top, a_bot, pw, inv = {}, {}, {}, {}, {}
    for key in keys:
        ci, h = key
        at_own[key] = jnp.where(owns[h], at[ci], 0.0)
        lhs = jnp.concatenate([at_own[key], jnp.where(owns[h], rt[ci], 0.0)], axis=0).astype(BF16)
        p = _dot_t(lhs, bt_kt[ci])
        a_top[key] = jnp.where(col < row, p[:c, :], 0.0)
        a_bot[key] = jnp.where(col <= row, p[c:, :], 0.0).astype(BF16)
        pw[key] = a_top[key][:, :c]
        inv[key] = eye + pw[key]
    span = 2
    while span < c:
        for key in keys:
            pw_b = pw[key].astype(BF16)
            pw[key] = _dot(pw_b, pw_b)
        for key in keys:
            inv[key] = inv[key] + _dot(pw[key].astype(BF16), inv[key].astype(BF16))
        span *= 2
    inv_b = {key: inv[key].astype(BF16) for key in keys}
    akv = {key: _dot(jnp.where(h0, 0.0, a_top[key]).astype(BF16), vv[key[0]]) for key in keys}
    at2h = {key: _dot(inv_b[key], at_own[key].astype(BF16)) for key in keys}
    uvh = {key: _dot(inv_b[key], akv[key].astype(BF16)) for key in keys}
    at2 = [(at2h[ci, 0] + at2h[ci, 1]).astype(BF16) for ci in chunks]
    uv = [jnp.where(h0, uvh[ci, 0], uvh[ci, 1]) for ci in chunks]
    w_step = [jnp.where(same_head, lax.dot_general(at2[ci], bh_kh[ci][:c], _CONTRACT_FIRST,
                                                   preferred_element_type=F32), 0.0).astype(BF16) for ci in chunks]
    n_step = [jnp.where(same_head, lax.dot_general(jnp.concatenate([uv[ci], v[ci]], axis=0).astype(BF16), bh_kh[ci],
                                                   _CONTRACT_FIRST, preferred_element_type=F32), 0.0) for ci in chunks]
    state = s_ref[...]
    entering = []
    for ci in chunks:
        state_b = state.astype(BF16)
        entering.append(state_b)
        state = state * e_tot[ci] + _dot(state_b, w_step[ci]) + n_step[ci]
    s_ref[...] = state
    u = [_dot_t(at2[ci], entering[ci]) + uv[ci] for ci in chunks]
    y_state = [_dot_t(rt[ci].astype(BF16), entering[ci]) for ci in chunks]
    for ci in chunks:
        u_v = jnp.concatenate([u[ci], v[ci]], axis=0).astype(BF16)
        y = y_state[ci] + jnp.where(h0, _dot(a_bot[ci, 0], u_v), _dot(a_bot[ci, 1], u_v))
        mu = head_sum(y) * (1.0 / HEAD_DIM)
        yc = y - mu
        var = head_sum(yc * yc) * (1.0 / HEAD_DIM)
        yn = yc * lax.rsqrt(var + GN_EPS) * gn_g + gn_b
        o_ref[sl[ci], :] = yn + head_sum(r_all[sl[ci]] * kh[ci] * r_k) * v[ci]


WKV_BLOCK = 1024


def _wkv(r, k, v, lw, a, k_k, k_a, r_k, gn_g, gn_b, batch, seq_len):
    t, d = r.shape
    assert seq_len % WKV_BLOCK == 0 and WKV_BLOCK % WKV_CHUNK == 0
    nt = seq_len // WKV_BLOCK
    tok = lambda bi, hp, ti: (bi * nt + ti, hp)
    par = lambda bi, hp, ti: (0, hp)
    vec = lambda z: z.reshape(1, d)
    return pl.pallas_call(
        functools.partial(_wkv_kernel, n_chunks=WKV_BLOCK // WKV_CHUNK),
        grid=(batch, N_PAIRS, nt),
        in_specs=[pl.BlockSpec((WKV_BLOCK, LANES), tok)] * 5 + [pl.BlockSpec((1, LANES), par)] * 5,
        out_specs=pl.BlockSpec((WKV_BLOCK, LANES), tok),
        out_shape=jax.ShapeDtypeStruct((t, d), F32),
        scratch_shapes=[pltpu.VMEM((LANES, LANES), F32)],
        compiler_params=_params("parallel", "parallel", "arbitrary"),
    )(r, k, v, lw, a, vec(k_k), vec(k_a), vec(r_k), vec(gn_g), vec(gn_b))


def _rwkv_layer(xt, b, s, v_first, mix, w_rkv, w0, w1, w2, a0, a1, a2, g1, g2, k_k, k_a, r_k, gn_g, gn_b, w_o, vres,
                ln_g, ln_b):
    r, k, v, lw, a, g = _rwkv_proj(xt, s, v_first, mix, w_rkv, w0, w1, w2, a0, a1, a2, g1, g2, vres)
    y = _wkv(r, k, v, lw, a, k_k, k_a, r_k, gn_g, gn_b, b, s)
    return _proj_ln(xt, y, w_o.astype(BF16), ln_g, ln_b, mult=g), (v if vres is None else v_first)


def _route_kernel(x_ref, w_ref, b_ref, pos_ref, gate_ref, off_ref, cnt_ref, *, tt):
    xh, xm, _ = _split3(x_ref[...])
    w = w_ref[...]
    wh = w.astype(BF16)
    wm = (w - wh.astype(F32)).astype(BF16)
    scores = jax.nn.sigmoid(_dot_t(wh, xh) + _dot_t(wh, xm) + _dot_t(wm, xh))
    biased = scores + b_ref[...]
    per = N_EXPERTS // N_GROUPS
    neg = -jnp.inf
    row_g = lax.broadcasted_iota(jnp.int32, (per, tt), 0)
    group_scores = []
    for g in range(N_GROUPS):
        grp = biased[g * per:(g + 1) * per, :]
        m1, i1 = _first_max(grp, row_g, per)
        m2 = jnp.max(jnp.where(row_g == i1, neg, grp), axis=0, keepdims=True)
        group_scores.append(m1 + m2)
    gsc = jnp.concatenate(group_scores, axis=0)
    row_n = lax.broadcasted_iota(jnp.int32, (N_GROUPS, tt), 0)
    keep = jnp.zeros((N_GROUPS, tt), F32)
    for _ in range(TOPK_GROUPS):
        _, ix = _first_max(gsc, row_n, N_GROUPS)
        pick = row_n == ix
        keep = jnp.where(pick, 1.0, keep)
        gsc = jnp.where(pick, neg, gsc)
    keep_e = jnp.concatenate([jnp.broadcast_to(keep[g:g + 1, :], (per, tt)) for g in range(N_GROUPS)], axis=0)
    cand = jnp.where(keep_e > 0.0, biased, neg)
    row_e = lax.broadcasted_iota(jnp.int32, (N_EXPERTS, tt), 0)
    picks = []
    for _ in range(TOP_K):
        _, ix = _first_max(cand, row_e, N_EXPERTS)
        pick = row_e == ix
        picks.append(pick)
        cand = jnp.where(pick, neg, cand)
    raw = [jnp.sum(jnp.where(p, scores, 0.0), axis=0, keepdims=True) for p in picks]
    total = raw[0]
    for r in raw[1:]:
        total = total + r
    gate_ref[0] = jnp.concatenate([r / total * ROUTED_SCALE for r in raw], axis=0)

    sel = jnp.zeros((N_EXPERTS, tt), F32)
    for p in picks:
        sel = jnp.where(p, 1.0, sel)
    cb = 256
    cr = lax.broadcasted_iota(jnp.int32, (cb, cb), 0)
    cc = lax.broadcasted_iota(jnp.int32, (cb, cb), 1)
    upper_incl = jnp.where(cr <= cc, 1.0, 0.0).astype(BF16)
    carry = jnp.zeros((N_EXPERTS, 1), F32)
    ranks = []
    for c0 in range(0, tt, cb):
        blk = sel[:, c0:c0 + cb]
        incl = _dot(blk.astype(BF16), upper_incl)
        ranks.append(incl - blk + carry)
        carry = carry + incl[:, cb - 1:cb]
    rank = jnp.concatenate(ranks, axis=1)
    count = jnp.broadcast_to(carry, (N_EXPERTS, LANES))
    aligned8 = jnp.floor((count + 7.0) * 0.125)
    er = lax.broadcasted_iota(jnp.int32, (N_EXPERTS, N_EXPERTS), 0)
    ec = lax.broadcasted_iota(jnp.int32, (N_EXPERTS, N_EXPERTS), 1)
    strict_lower = jnp.where(ec < er, 1.0, 0.0).astype(BF16)
    start = 8.0 * _dot(strict_lower, aligned8.astype(BF16))
    where_to = (start[:, :1] + rank) * float(D_MODEL // LANES)
    pos_ref[0] = jnp.concatenate(
        [jnp.sum(jnp.where(p, where_to, 0.0), axis=0, keepdims=True) for p in picks], axis=0).astype(jnp.int32)
    off_ref[0] = start.astype(jnp.int32)
    cnt_ref[0] = count.astype(jnp.int32)


MOE_TILE = 1024
MOE_CHUNK = 160
EXPERTS_PER_STEP = 2
SEG_ALIGN = 8
ROW_LOOP_UNROLL = 4


def _route_tiles(xt, w_router, router_bias):
    t, d = xt.shape
    tt = MOE_TILE
    n_tiles = t // tt
    tile3 = lambda i: (i, 0, 0)
    pos, gate, off, cnt = pl.pallas_call(
        functools.partial(_route_kernel, tt=tt),
        grid=(n_tiles,),
        in_specs=[pl.BlockSpec((tt, d), lambda i: (i, 0)),
                  pl.BlockSpec((N_EXPERTS, d), lambda i: (0, 0)),
                  pl.BlockSpec((N_EXPERTS, 1), lambda i: (0, 0))],
        out_specs=[pl.BlockSpec((1, TOP_K, tt), tile3), pl.BlockSpec((1, TOP_K, tt), tile3),
                   pl.BlockSpec((1, N_EXPERTS, LANES), tile3), pl.BlockSpec((1, N_EXPERTS, LANES), tile3)],
        out_shape=[jax.ShapeDtypeStruct((n_tiles, TOP_K, tt), jnp.int32),
                   jax.ShapeDtypeStruct((n_tiles, TOP_K, tt), F32),
                   jax.ShapeDtypeStruct((n_tiles, N_EXPERTS, LANES), jnp.int32),
                   jax.ShapeDtypeStruct((n_tiles, N_EXPERTS, LANES), jnp.int32)],
        compiler_params=_params("parallel"),
    )(xt, w_router.astype(F32).T, router_bias.astype(F32).reshape(N_EXPERTS, 1))
    token_major = lambda z: z.transpose(0, 2, 1).reshape(n_tiles, tt * TOP_K)
    return token_major(pos), token_major(gate), off[:, :, 0].reshape(-1), cnt[:, :, 0].reshape(-1)


def _moe_tile_kernel(off_ref, cnt_ref, nch_ref, x_hbm, pos_hbm, gate_hbm, wg_ref, wu_ref, wd_ref, o_hbm,
                     stage, tok, pos_s, gate_s, sems, *, tt):
    ti = pl.program_id(0)
    step = pl.program_id(1)
    sub = D_MODEL // LANES
    rows = MOE_CHUNK

    def tile_copy(src, dst, sem):
        return pltpu.make_async_copy(src, dst, sem)

    @pl.when(step == 0)
    def _distribute():
        copies = [tile_copy(x_hbm.at[pl.ds(pl.multiple_of(ti * tt * sub, tt * sub), tt * sub)], tok, sems.at[0]),
                  tile_copy(pos_hbm.at[ti], pos_s, sems.at[1]),
                  tile_copy(gate_hbm.at[ti], gate_s, sems.at[2])]
        for c in copies:
            c.start()
        for c in copies:
            c.wait()
        zero_group = jnp.zeros((SEG_ALIGN * sub, LANES), F32)

        def clear_tail(ex, carry):
            end = off_ref[ti * N_EXPERTS + ex] + cnt_ref[ti * N_EXPERTS + ex]
            g0 = pl.multiple_of(lax.shift_left(lax.shift_right_logical(end, 3), 3) * sub, SEG_ALIGN * sub)
            stage[pl.ds(g0, SEG_ALIGN * sub), :] = zero_group
            return carry

        lax.fori_loop(0, N_EXPERTS, clear_tail, 0)
        last = ti * N_EXPERTS + N_EXPERTS - 1
        total = off_ref[last] + lax.shift_left(lax.shift_right_logical(cnt_ref[last] + 7, 3), 3)
        t0 = pl.multiple_of(total * sub, SEG_ALIGN * sub)
        stage[pl.ds(t0, rows * sub), :] = jnp.zeros((rows * sub, LANES), F32)

        def place(tq, carry):
            for t in [ROW_LOOP_UNROLL * tq + u for u in range(ROW_LOOP_UNROLL)]:
                row = tok[pl.ds(pl.multiple_of(t * sub, sub), sub), :]
                for j in range(TOP_K):
                    stage[pl.ds(pl.multiple_of(pos_s[t * TOP_K + j], sub), sub), :] = row
            return carry

        lax.fori_loop(0, tt // ROW_LOOP_UNROLL, place, 0)

    row_id = lax.broadcasted_iota(jnp.int32, (rows, LANES), 0)
    experts = []
    for i in range(EXPERTS_PER_STEP):
        e = (ti * (N_EXPERTS // EXPERTS_PER_STEP) + step) * EXPERTS_PER_STEP + i
        experts.append((i, off_ref[e], cnt_ref[e], nch_ref[e]))
    n_steps = experts[0][3]
    for ex in experts[1:]:
        n_steps = jnp.maximum(n_steps, ex[3])

    def ffn_step(c, carry):
        loaded = []
        for i, off, cnt, n_mine in experts:
            c_mine = jnp.minimum(c, jnp.maximum(n_mine - 1, 0))
            base = pl.multiple_of((off + c_mine * rows) * sub, SEG_ALIGN * sub)
            loaded.append((base, [stage[pl.ds(base + k, rows, stride=sub), :] for k in range(sub)]))
        outs = []
        for (i, off, cnt, n_mine), (base, parts) in zip(experts, loaded):
            xb = jnp.concatenate(parts, axis=1).astype(BF16)
            h = jax.nn.silu(_dot(xb, wg_ref[i])) * _dot(xb, wu_ref[i])
            outs.append(_dot(h.astype(BF16), wd_ref[i]))
        for (i, off, cnt, n_mine), (base, parts), y in zip(experts, loaded, outs):
            live = row_id < cnt - c * rows
            for k in range(sub):
                stage[pl.ds(base + k, rows, stride=sub), :] = jnp.where(live, y[:, k * LANES:(k + 1) * LANES], parts[k])
        return carry

    lax.fori_loop(0, n_steps, ffn_step, 0)

    @pl.when(step == N_EXPERTS // EXPERTS_PER_STEP - 1)
    def _combine():
        def gather(tq, carry):
            for t in [ROW_LOOP_UNROLL * tq + u for u in range(ROW_LOOP_UNROLL)]:
                acc = jnp.zeros((sub, LANES), F32)
                for j in range(TOP_K):
                    p = pl.multiple_of(pos_s[t * TOP_K + j], sub)
                    acc = acc + gate_s[t * TOP_K + j] * stage[pl.ds(p, sub), :]
                tok[pl.ds(pl.multiple_of(t * sub, sub), sub), :] = acc
            return carry

        lax.fori_loop(0, tt // ROW_LOOP_UNROLL, gather, 0)
        out = tile_copy(tok, o_hbm.at[pl.ds(pl.multiple_of(ti * tt * sub, tt * sub), tt * sub)], sems.at[0])
        out.start()
        out.wait()


def _routed_experts(x8, pos, gate, off, cnt, w_gate, w_up, w_down):
    sub = SUBROWS
    t, d = x8.shape[0] // sub, D_MODEL
    tt = MOE_TILE
    n_tiles = t // tt
    stage_rows = TOP_K * tt + N_EXPERTS * SEG_ALIGN + MOE_CHUNK
    any_spec = pl.BlockSpec(memory_space=pl.ANY)
    grid_spec = pltpu.PrefetchScalarGridSpec(
        num_scalar_prefetch=3,
        grid=(n_tiles, N_EXPERTS // EXPERTS_PER_STEP),
        in_specs=[any_spec, any_spec, any_spec,
                  pl.BlockSpec((EXPERTS_PER_STEP, d, D_EXPERT), lambda ti, e, *_: (e, 0, 0)),
                  pl.BlockSpec((EXPERTS_PER_STEP, d, D_EXPERT), lambda ti, e, *_: (e, 0, 0)),
                  pl.BlockSpec((EXPERTS_PER_STEP, D_EXPERT, d), lambda ti, e, *_: (e, 0, 0))],
        out_specs=any_spec,
        scratch_shapes=[pltpu.VMEM((stage_rows * sub, LANES), F32),
                        pltpu.VMEM((tt * sub, LANES), F32),
                        pltpu.SMEM((TOP_K * tt,), jnp.int32),
                        pltpu.SMEM((TOP_K * tt,), F32),
                        pltpu.SemaphoreType.DMA((3,))],
    )
    return pl.pallas_call(
        functools.partial(_moe_tile_kernel, tt=tt),
        grid_spec=grid_spec,
        out_shape=jax.ShapeDtypeStruct((t * sub, LANES), F32),
        compiler_params=_params("arbitrary", "arbitrary"),
    )(off, cnt, (cnt + MOE_CHUNK - 1) // MOE_CHUNK, x8, pos, gate, w_gate, w_up, w_down)


def _shared_ln_kernel(x_ref, r8_ref, sg_ref, su_ref, sd_ref, g_ref, b_ref, o_ref):
    x = x_ref[...]
    xb = x.astype(BF16)
    h = jax.nn.silu(_dot(xb, sg_ref[...])) * _dot(xb, su_ref[...])
    f = _load_row_major(r8_ref, x.shape[0]) + _dot(h.astype(BF16), sd_ref[...])
    o_ref[...] = _layer_norm_rows(ALPHA * x + f, g_ref[...], b_ref[...])


def _shared_ln(xt, routed8, sh_gate, sh_up, sh_down, g, b):
    t, d = xt.shape
    row = lambda i: (i, 0)
    fix = lambda i: (0, 0)
    return pl.pallas_call(
        _shared_ln_kernel,
        grid=(t // ROW_BLOCK,),
        in_specs=[pl.BlockSpec((ROW_BLOCK, d), row), pl.BlockSpec((ROW_BLOCK * SUBROWS, LANES), row),
                  pl.BlockSpec((d, D_EXPERT), fix), pl.BlockSpec((d, D_EXPERT), fix), pl.BlockSpec((D_EXPERT, d), fix),
                  pl.BlockSpec((1, d), fix), pl.BlockSpec((1, d), fix)],
        out_specs=pl.BlockSpec((ROW_BLOCK, d), row),
        out_shape=jax.ShapeDtypeStruct((t, d), F32),
        compiler_params=_params("parallel"),
    )(xt, routed8, sh_gate.astype(BF16), sh_up.astype(BF16), sh_down.astype(BF16), g.reshape(1, d), b.reshape(1, d))


def _moe_layer(xt, x8, w_router, router_bias, w_gate, w_up, w_down, sh_gate, sh_up, sh_down, ln_g, ln_b):
    assert xt.shape[0] % MOE_TILE == 0
    pos, gate, off, cnt = _route_tiles(xt, w_router, router_bias)
    routed8 = _routed_experts(x8, pos, gate, off, cnt, w_gate.astype(BF16), w_up.astype(BF16), w_down.astype(BF16))
    return _shared_ln(xt, routed8, sh_gate, sh_up, sh_down, ln_g, ln_b)


def kernel(x, moba_w_qkv, moba_w_o, rel_bias, rwkv_mix, rwkv_w_rkv, rwkv_w0, rwkv_w1, rwkv_w2,
           rwkv_a0, rwkv_a1, rwkv_a2, rwkv_v0, rwkv_v1, rwkv_v2, rwkv_g1, rwkv_g2, rwkv_k_k,
           rwkv_k_a, rwkv_r_k, rwkv_gn_g, rwkv_gn_b, rwkv_w_o, moe_w_router, moe_router_bias,
           moe_w_gate, moe_w_up, moe_w_down, moe_sh_gate, moe_sh_up, moe_sh_down,
           ln_mix_g, ln_mix_b, ln_ffn_g, ln_ffn_b):
    assert x.shape[-1] == D_MODEL
    own_bias, prev_bias = _moba_bias_tables(rel_bias)
    b, s, d = x.shape
    xt = x.reshape(b * s, d)
    v_first = None
    for i in range(DEPTH):
        j = i // 2
        if i % 2 == 0:
            xt, x8 = _moba_layer(xt, b, s, moba_w_qkv[j], moba_w_o[j], own_bias, prev_bias, ln_mix_g[i], ln_mix_b[i])
        else:
            vres = None if j == 0 else (rwkv_v0[j - 1], rwkv_v1[j - 1], rwkv_v2[j - 1])
            (xt, x8), v_first = _rwkv_layer(xt, b, s, v_first, rwkv_mix[j], rwkv_w_rkv[j], rwkv_w0[j], rwkv_w1[j],
                                            rwkv_w2[j], rwkv_a0[j], rwkv_a1[j], rwkv_a2[j], rwkv_g1[j],
                                            rwkv_g2[j], rwkv_k_k[j], rwkv_k_a[j], rwkv_r_k[j],
                                            rwkv_gn_g[j], rwkv_gn_b[j], rwkv_w_o[j], vres, ln_mix_g[i], ln_mix_b[i])
        xt = _moe_layer(xt, x8, moe_w_router[i], moe_router_bias[i], moe_w_gate[i], moe_w_up[i], moe_w_down[i],
                        moe_sh_gate[i], moe_sh_up[i], moe_sh_down[i], ln_ffn_g[i], ln_ffn_b[i])
    return xt.reshape(b, s, d)
```
